```python
import math
import jax, jax.numpy as jnp
from jax import lax
import numpy as np

D_MODEL = 2048
BATCH = 4
SEQ = 2048
DEPTH = 2
DEC_BATCH = 32
DEC_SEQ = 1
PAST_LEN = 8192
PAGE_SIZE = 128

N_MIXERS = 2
N_LRU_LAYERS = (DEPTH + 1) // 2
N_ATTN_LAYERS = DEPTH // 2
LRU_WIDTH = D_MODEL
LRU_BLOCK = 256
LRU_BLOCKS = LRU_WIDTH // LRU_BLOCK
CONV_WIDTH = 4
LRU_C = 8.0
HEAD_DIM = 128
N_HEADS = D_MODEL // (2 * HEAD_DIM)
ROPE_THETA = 10000.0
Q_BLOCK = 128
D_FF = ((-(-8 * D_MODEL // 3) + 255) // 256) * 256
NORM_EPS = 1e-6
SUBLN_EPS = 1e-5

kernel_name = 'hybrid_rglru_diffattn_decode_step'


def rms_norm(x, g, eps=NORM_EPS):
    xf = x.astype(jnp.float32)
    y = xf * lax.rsqrt(jnp.mean(xf * xf, axis=-1, keepdims=True) + eps) * g.astype(jnp.float32)
    return y.astype(x.dtype)


def swiglu(h, w_gate, w_up, w_down):
    return (jax.nn.silu(h @ w_gate) * (h @ w_up)) @ w_down


def rglru_mixer(u, conv_buf, h0, w_in, b_in, conv_w, conv_b, w_ga, b_ga, w_gx, b_gx, a_param, w_out):
    B, T, _ = u.shape
    proj = u @ w_in + b_in
    y_gate = jax.nn.gelu(proj[..., :LRU_WIDTH])
    xr = proj[..., LRU_WIDTH:]
    xcat = jnp.concatenate([conv_buf.astype(xr.dtype), xr], axis=1)
    xc = conv_b + xcat[:, 0:T] * conv_w[0]
    for tap in range(1, CONV_WIDTH):
        xc = xc + xcat[:, tap:tap + T] * conv_w[tap]
    new_buf = xcat[:, T:]
    xb = xc.reshape(B, T, LRU_BLOCKS, LRU_BLOCK)
    gate_a = jax.nn.sigmoid(jnp.einsum('btnj,njk->btnk', xb, w_ga) + b_ga).reshape(B, T, LRU_WIDTH)
    gate_x = jax.nn.sigmoid(jnp.einsum('btnj,njk->btnk', xb, w_gx) + b_gx).reshape(B, T, LRU_WIDTH)
    log_a = -LRU_C * gate_a.astype(jnp.float32) * jax.nn.softplus(-a_param.astype(jnp.float32))
    a = jnp.exp(log_a)
    b = jnp.sqrt(-jnp.expm1(2.0 * log_a)) * (gate_x * xc).astype(jnp.float32)

    def step(h, ab):
        a_t, b_t = ab
        h = a_t * h + b_t
        return h, h

    h_last, hs = lax.scan(step, h0.astype(jnp.float32), (a.transpose(1, 0, 2), b.transpose(1, 0, 2)))
    hs = hs.transpose(1, 0, 2).astype(u.dtype)
    out = (hs * y_gate) @ w_out
    return out, new_buf, h_last.astype(u.dtype)


def rope(x, pos):
    half = HEAD_DIM // 2
    inv_freq = jnp.power(ROPE_THETA, -jnp.arange(half, dtype=jnp.float32) * (2.0 / HEAD_DIM))
    ang = pos.astype(jnp.float32)[:, None] * inv_freq[None, :]
    cos = jnp.cos(ang)[None, :, None, :]
    sin = jnp.sin(ang)[None, :, None, :]
    xf = x.astype(jnp.float32)
    x1, x2 = xf[..., :half], xf[..., half:]
    return jnp.concatenate([x1 * cos - x2 * sin, x2 * cos + x1 * sin], axis=-1).astype(x.dtype)


def diff_qkv(h, w_qkv, pos):
    B, T, _ = h.shape
    q, k, v = jnp.split(h @ w_qkv, 3, axis=-1)
    q = rope(q.reshape(B, T, 2 * N_HEADS, HEAD_DIM), pos)
    k = rope(k.reshape(B, T, 2 * N_HEADS, HEAD_DIM), pos)
    v = v.reshape(B, T, N_HEADS, 2 * HEAD_DIM)
    return q, k, v


def diff_attend(q, segments, lam):
    qf = q.astype(jnp.float32) * (HEAD_DIM ** -0.5)
    scores = [jnp.where(m[None, None], jnp.einsum('bqhd,bkhd->bhqk', qf, k.astype(jnp.float32)), -jnp.inf)
              for k, _, m in segments]
    p = jax.nn.softmax(jnp.concatenate(scores, axis=-1), axis=-1)
    B, _, Qn, Kt = p.shape
    p = p.reshape(B, N_HEADS, 2, Qn, Kt)
    w = p[:, :, 0] - lam * p[:, :, 1]
    out = None
    start = 0
    for k, v, _ in segments:
        n = k.shape[1]
        part = jnp.einsum('bhqk,bkhe->bqhe', w[..., start:start + n], v.astype(jnp.float32))
        out = part if out is None else out + part
        start += n
    return out


def diff_output(o, lam_init, subln_g, w_o, dtype):
    o = rms_norm(o, subln_g, SUBLN_EPS) * (1.0 - lam_init)
    B, T = o.shape[:2]
    return o.reshape(B, T, N_HEADS * 2 * HEAD_DIM).astype(dtype) @ w_o


def setup_inputs(seed: int = 0) -> dict:
    key = jax.random.key(seed)
    ks = iter(jax.random.split(key, 48))

    def nrm(shape, scale=1.0):
        return jax.random.normal(next(ks), shape, jnp.float32) * scale

    n_pages = PAST_LEN // PAGE_SIZE
    used = DEC_BATCH * n_pages
    pool = used + (used + 3) // 4
    x_prompt = nrm((BATCH, SEQ, D_MODEL))
    x_sample = nrm((DEC_BATCH, DEC_SEQ, D_MODEL))
    state_conv = nrm((N_LRU_LAYERS, DEC_BATCH, CONV_WIDTH - 1, LRU_WIDTH))
    state_h = nrm((N_LRU_LAYERS, DEC_BATCH, LRU_WIDTH), 0.5)
    cache_k = nrm((N_ATTN_LAYERS, pool, PAGE_SIZE, 2 * N_HEADS, HEAD_DIM))
    cache_v = nrm((N_ATTN_LAYERS, pool, PAGE_SIZE, N_HEADS, 2 * HEAD_DIM))
    page_table = jax.random.permutation(next(ks), pool)[:used].reshape(DEC_BATCH, n_pages).astype(jnp.int32)
    norm_mix_pre = 1.0 + nrm((DEPTH, D_MODEL), 0.02)
    norm_mix_post = 1.0 + nrm((DEPTH, D_MODEL), 0.02)
    norm_ffn_pre = 1.0 + nrm((DEPTH, D_MODEL), 0.02)
    norm_ffn_post = 1.0 + nrm((DEPTH, D_MODEL), 0.02)
    lru_w_in = nrm((N_LRU_LAYERS, D_MODEL, 2 * LRU_WIDTH), D_MODEL ** -0.5)
    lru_b_in = nrm((N_LRU_LAYERS, 2 * LRU_WIDTH), 0.01)
    lru_conv_w = nrm((N_LRU_LAYERS, CONV_WIDTH, LRU_WIDTH), CONV_WIDTH ** -0.5)
    lru_conv_b = nrm((N_LRU_LAYERS, LRU_WIDTH), 0.01)
    lru_w_gate_a = nrm((N_LRU_LAYERS, LRU_BLOCKS, LRU_BLOCK, LRU_BLOCK), LRU_BLOCK ** -0.5)
    lru_b_gate_a = nrm((N_LRU_LAYERS, LRU_BLOCKS, LRU_BLOCK), 0.01)
    lru_w_gate_x = nrm((N_LRU_LAYERS, LRU_BLOCKS, LRU_BLOCK, LRU_BLOCK), LRU_BLOCK ** -0.5)
    lru_b_gate_x = nrm((N_LRU_LAYERS, LRU_BLOCKS, LRU_BLOCK), 0.01)
    u = jax.random.uniform(next(ks), (N_LRU_LAYERS, LRU_WIDTH), jnp.float32, 0.9, 0.999)
    lru_a_param = jnp.log(u) - jnp.log1p(-u)
    lru_w_out = nrm((N_LRU_LAYERS, LRU_WIDTH, D_MODEL), LRU_WIDTH ** -0.5)
    attn_w_qkv = nrm((N_ATTN_LAYERS, D_MODEL, 3 * D_MODEL), D_MODEL ** -0.5)
    attn_lambda_q1 = nrm((N_ATTN_LAYERS, HEAD_DIM), 0.1)
    attn_lambda_k1 = nrm((N_ATTN_LAYERS, HEAD_DIM), 0.1)
    attn_lambda_q2 = nrm((N_ATTN_LAYERS, HEAD_DIM), 0.1)
    attn_lambda_k2 = nrm((N_ATTN_LAYERS, HEAD_DIM), 0.1)
    attn_subln = 1.0 + nrm((N_ATTN_LAYERS, 2 * HEAD_DIM), 0.02)
    attn_w_o = nrm((N_ATTN_LAYERS, D_MODEL, D_MODEL), D_MODEL ** -0.5)
    ffn_w_gate = nrm((DEPTH, D_MODEL, D_FF), D_MODEL ** -0.5)
    ffn_w_up = nrm((DEPTH, D_MODEL, D_FF), D_MODEL ** -0.5)
    ffn_w_down = nrm((DEPTH, D_FF, D_MODEL), D_FF ** -0.5)
    return {
        'x_prompt': x_prompt, 'x_sample': x_sample,
        'state_conv': state_conv, 'state_h': state_h,
        'cache_k': cache_k, 'cache_v': cache_v, 'page_table': page_table,
        'norm_mix_pre': norm_mix_pre, 'norm_mix_post': norm_mix_post,
        'norm_ffn_pre': norm_ffn_pre, 'norm_ffn_post': norm_ffn_post,
        'lru_w_in': lru_w_in, 'lru_b_in': lru_b_in, 'lru_conv_w': lru_conv_w, 'lru_conv_b': lru_conv_b,
        'lru_w_gate_a': lru_w_gate_a, 'lru_b_gate_a': lru_b_gate_a,
        'lru_w_gate_x': lru_w_gate_x, 'lru_b_gate_x': lru_b_gate_x,
        'lru_a_param': lru_a_param, 'lru_w_out': lru_w_out,
        'attn_w_qkv': attn_w_qkv, 'attn_lambda_q1': attn_lambda_q1, 'attn_lambda_k1': attn_lambda_k1,
        'attn_lambda_q2': attn_lambda_q2, 'attn_lambda_k2': attn_lambda_k2,
        'attn_subln': attn_subln, 'attn_w_o': attn_w_o,
        'ffn_w_gate': ffn_w_gate, 'ffn_w_up': ffn_w_up, 'ffn_w_down': ffn_w_down,
    }


def reference(x_prompt, x_sample, state_conv, state_h, cache_k, cache_v, page_table,
              norm_mix_pre, norm_mix_post, norm_ffn_pre, norm_ffn_post,
              lru_w_in, lru_b_in, lru_conv_w, lru_conv_b, lru_w_gate_a, lru_b_gate_a,
              lru_w_gate_x, lru_b_gate_x, lru_a_param, lru_w_out,
              attn_w_qkv, attn_lambda_q1, attn_lambda_k1, attn_lambda_q2, attn_lambda_k2,
              attn_subln, attn_w_o, ffn_w_gate, ffn_w_up, ffn_w_down):
    xp, xs = x_prompt, x_sample
    Bp, S = xp.shape[:2]
    Bd, T = xs.shape[:2]
    n_pages = page_table.shape[1]
    past = n_pages * cache_k.shape[2]
    pos_p = jnp.arange(S)
    pos_s = past + jnp.arange(T)
    conv_p, h_p, k_p, v_p = [], [], [], []
    conv_s, h_s, k_s, v_s = [], [], [], []
    for i in range(DEPTH):
        j = i // N_MIXERS
        hp = rms_norm(xp, norm_mix_pre[i])
        hs = rms_norm(xs, norm_mix_pre[i])
        if i % N_MIXERS == 0:
            prm = (lru_w_in[j], lru_b_in[j], lru_conv_w[j], lru_conv_b[j], lru_w_gate_a[j], lru_b_gate_a[j],
                   lru_w_gate_x[j], lru_b_gate_x[j], lru_a_param[j], lru_w_out[j])
            mp, cb_p, hl_p = rglru_mixer(hp, jnp.zeros((Bp, CONV_WIDTH - 1, LRU_WIDTH), hp.dtype),
                                         jnp.zeros((Bp, LRU_WIDTH), jnp.float32), *prm)
            ms, cb_s, hl_s = rglru_mixer(hs, state_conv[j], state_h[j], *prm)
            conv_p.append(cb_p); h_p.append(hl_p)
            conv_s.append(cb_s); h_s.append(hl_s)
        else:
            lam_init = 0.8 - 0.6 * math.exp(-0.3 * i)
            lam = (jnp.exp(jnp.sum(attn_lambda_q1[j].astype(jnp.float32) * attn_lambda_k1[j].astype(jnp.float32)))
                   - jnp.exp(jnp.sum(attn_lambda_q2[j].astype(jnp.float32) * attn_lambda_k2[j].astype(jnp.float32)))
                   + lam_init)
            q, k, v = diff_qkv(hp, attn_w_qkv[j], pos_p)

            def q_block(bi, q=q, k=k, v=v, lam=lam):
                qb = lax.dynamic_slice_in_dim(q, bi * Q_BLOCK, Q_BLOCK, axis=1)
                qpos = bi * Q_BLOCK + jnp.arange(Q_BLOCK)
                mask = pos_p[None, :] <= qpos[:, None]
                return diff_attend(qb, ((k, v, mask),), lam)

            o = lax.map(q_block, jnp.arange(S // Q_BLOCK))
            o = o.transpose(1, 0, 2, 3, 4).reshape(Bp, S, N_HEADS, 2 * HEAD_DIM)
            mp = diff_output(o, lam_init, attn_subln[j], attn_w_o[j], hp.dtype)
            k_p.append(k); v_p.append(v)
            qs, ks_new, vs_new = diff_qkv(hs, attn_w_qkv[j], pos_s)
            k_past = cache_k[j, page_table].reshape(Bd, past, 2 * N_HEADS, HEAD_DIM)
            v_past = cache_v[j, page_table].reshape(Bd, past, N_HEADS, 2 * HEAD_DIM)
            mask_past = jnp.ones((T, past), dtype=bool)
            mask_new = jnp.arange(T)[None, :] <= jnp.arange(T)[:, None]
            os_ = diff_attend(qs, ((k_past, v_past, mask_past), (ks_new, vs_new, mask_new)), lam)
            ms = diff_output(os_, lam_init, attn_subln[j], attn_w_o[j], hs.dtype)
            k_s.append(ks_new); v_s.append(vs_new)
        xp = xp + rms_norm(mp, norm_mix_post[i])
        xs = xs + rms_norm(ms, norm_mix_post[i])
        xp = xp + rms_norm(swiglu(rms_norm(xp, norm_ffn_pre[i]), ffn_w_gate[i], ffn_w_up[i], ffn_w_down[i]), norm_ffn_post[i])
        xs = xs + rms_norm(swiglu(rms_norm(xs, norm_ffn_pre[i]), ffn_w_gate[i], ffn_w_up[i], ffn_w_down[i]), norm_ffn_post[i])
    return (xp, xs, jnp.stack(conv_p), jnp.stack(h_p), jnp.stack(k_p), jnp.stack(v_p),
            jnp.stack(conv_s), jnp.stack(h_s), jnp.stack(k_s), jnp.stack(v_s))
```

```python
import functools
import math

import jax
import jax.numpy as jnp
from jax import lax
from jax.experimental import pallas as pl
from jax.experimental.pallas import tpu as pltpu

D_MODEL = 2048
LRU_WIDTH = D_MODEL
LRU_BLOCK = 256
LRU_BLOCKS = LRU_WIDTH // LRU_BLOCK
CONV_WIDTH = 4
LRU_C = 8.0
HEAD_DIM = 128
N_HEADS = D_MODEL // (2 * HEAD_DIM)
ROPE_THETA = 10000.0
NORM_EPS = 1e-6
SUBLN_EPS = 1e-5
N_MIXERS = 2

SUBLANES = 8
LANES = 128
VMEM_LIMIT_BYTES = 56 * 1024 * 1024

BF16 = jnp.bfloat16
F32 = jnp.float32


def _params(n_grid_dims):
    return pltpu.CompilerParams(
        dimension_semantics=("arbitrary",) * n_grid_dims,
        vmem_limit_bytes=VMEM_LIMIT_BYTES,
    )


def _rms_scale(x, eps):
    return lax.rsqrt(jnp.mean(x * x, axis=-1, keepdims=True) + eps)


def _sigmoid(x):
    return 1.0 / (1.0 + jnp.exp(-x))


def _gelu_tanh(x):
    c = math.sqrt(2.0 / math.pi)
    return 0.5 * x * (1.0 + jnp.tanh(c * (x + 0.044715 * (x * x * x))))


def _softplus(z):
    return jnp.maximum(z, 0.0) + jnp.log1p(jnp.exp(-jnp.abs(z)))


def _norm_matmul_body(x_ref, g_ref, w_ref, h_ref):
    @pl.when(pl.program_id(1) == 0)
    def _():
        x = x_ref[...]
        h_ref[...] = (x * _rms_scale(x, NORM_EPS) * g_ref[...]).astype(BF16)

    return jnp.dot(h_ref[...], w_ref[...], preferred_element_type=F32)


def _lru_in_kernel(x_ref, g_ref, w_ref, b_ref, o_ref, h_ref, *, n_gate_tiles):
    acc = _norm_matmul_body(x_ref, g_ref, w_ref, h_ref) + b_ref[...]
    j = pl.program_id(1)

    @pl.when(j < n_gate_tiles)
    def _():
        o_ref[...] = _gelu_tanh(acc)

    @pl.when(j >= n_gate_tiles)
    def _():
        o_ref[...] = acc


def _lru_in_proj(x, g, w, b, *, tm, tn):
    m, d = x.shape
    n = w.shape[1]
    return pl.pallas_call(
        functools.partial(_lru_in_kernel, n_gate_tiles=LRU_WIDTH // tn),
        grid=(m // tm, n // tn),
        in_specs=[
            pl.BlockSpec((tm, d), lambda i, j: (i, 0)),
            pl.BlockSpec((1, d), lambda i, j: (0, 0)),
            pl.BlockSpec((d, tn), lambda i, j: (0, j)),
            pl.BlockSpec((1, tn), lambda i, j: (0, j)),
        ],
        out_specs=pl.BlockSpec((tm, tn), lambda i, j: (i, j)),
        out_shape=jax.ShapeDtypeStruct((m, n), F32),
        scratch_shapes=[pltpu.VMEM((tm, d), BF16)],
        compiler_params=_params(2),
        name="lru_in_proj",
    )(x, g, w, b)


def _rope(x, cos2, sin2):
    return x * cos2 + pltpu.roll(x, HEAD_DIM // 2, axis=1) * sin2


def _qkv_kernel(x_ref, g_ref, w_ref, cos_ref, sin_ref, q_ref, k_ref, v_ref, h_ref, *, tiles_per_part, heads_per_tile):
    acc = _norm_matmul_body(x_ref, g_ref, w_ref, h_ref)
    j = pl.program_id(1)

    def roped(scale):
        cos2 = cos_ref[...]
        sin2 = sin_ref[...]
        parts = []
        for hd in range(heads_per_tile):
            xh = acc[:, hd * HEAD_DIM:(hd + 1) * HEAD_DIM]
            r = _rope(xh, cos2, sin2)
            parts.append(r if scale is None else r * scale)
        return jnp.concatenate(parts, axis=1)

    @pl.when(j < tiles_per_part)
    def _():
        q_ref[...] = roped(HEAD_DIM ** -0.5).astype(q_ref.dtype)

    @pl.when((j >= tiles_per_part) & (j < 2 * tiles_per_part))
    def _():
        k_ref[...] = roped(None)

    @pl.when(j >= 2 * tiles_per_part)
    def _():
        v_ref[...] = acc


def _qkv_proj(x, g, w, cos2, sin2, *, tm, tn):
    m, d = x.shape
    n = w.shape[1]
    tpp = d // tn
    n_pos_tiles = cos2.shape[0] // tm
    last = tpp - 1
    return pl.pallas_call(
        functools.partial(_qkv_kernel, tiles_per_part=tpp, heads_per_tile=tn // HEAD_DIM),
        grid=(m // tm, n // tn),
        in_specs=[
            pl.BlockSpec((tm, d), lambda i, j: (i, 0)),
            pl.BlockSpec((1, d), lambda i, j: (0, 0)),
            pl.BlockSpec((d, tn), lambda i, j: (0, j)),
            pl.BlockSpec((tm, HEAD_DIM), lambda i, j: (i % n_pos_tiles, 0)),
            pl.BlockSpec((tm, HEAD_DIM), lambda i, j: (i % n_pos_tiles, 0)),
        ],
        out_specs=[
            pl.BlockSpec((tm, tn), lambda i, j: (i, jnp.minimum(j, last))),
            pl.BlockSpec((tm, tn), lambda i, j: (i, jnp.clip(j - tpp, 0, last))),
            pl.BlockSpec((tm, tn), lambda i, j: (i, jnp.clip(j - 2 * tpp, 0, last))),
        ],
        out_shape=[
            jax.ShapeDtypeStruct((m, d), BF16),
            jax.ShapeDtypeStruct((m, d), F32),
            jax.ShapeDtypeStruct((m, d), F32),
        ],
        scratch_shapes=[pltpu.VMEM((tm, d), BF16)],
        compiler_params=_params(2),
        name="qkv_proj",
    )(x, g, w, cos2, sin2)


def _out_proj_kernel(a_ref, w_ref, g_ref, x_ref, o_ref):
    y = jnp.dot(a_ref[...].astype(BF16), w_ref[...], preferred_element_type=F32)
    o_ref[...] = x_ref[...] + y * _rms_scale(y, NORM_EPS) * g_ref[...]


def _out_proj_residual(a, w, g, x, *, tm):
    m, k = a.shape
    d = w.shape[1]
    return pl.pallas_call(
        _out_proj_kernel,
        grid=(m // tm,),
        in_specs=[
            pl.BlockSpec((tm, k), lambda i: (i, 0)),
            pl.BlockSpec((k, d), lambda i: (0, 0)),
            pl.BlockSpec((1, d), lambda i: (0, 0)),
            pl.BlockSpec((tm, d), lambda i: (i, 0)),
        ],
        out_specs=pl.BlockSpec((tm, d), lambda i: (i, 0)),
        out_shape=jax.ShapeDtypeStruct((m, d), F32),
        compiler_params=_params(1),
        name="out_proj_residual",
    )(a, w, g, x)


def _ffn_kernel(x_ref, gpre_ref, wg_ref, wu_ref, wd_ref, gpost_ref, o_ref, h_ref):
    f = pl.program_id(1)

    @pl.when(f == 0)
    def _():
        x = x_ref[...]
        h_ref[...] = (x * _rms_scale(x, NORM_EPS) * gpre_ref[...]).astype(BF16)

    h = h_ref[...]
    gate = jnp.dot(h, wg_ref[...], preferred_element_type=F32)
    up = jnp.dot(h, wu_ref[...], preferred_element_type=F32)
    act = (gate * _sigmoid(gate) * up).astype(BF16)
    part = jnp.dot(act, wd_ref[...], preferred_element_type=F32)

    @pl.when(f == 0)
    def _():
        o_ref[...] = part

    @pl.when(f > 0)
    def _():
        o_ref[...] += part

    @pl.when(f == pl.num_programs(1) - 1)
    def _():
        y = o_ref[...]
        o_ref[...] = x_ref[...] + y * _rms_scale(y, NORM_EPS) * gpost_ref[...]


def _ffn(x, gpre, wg, wu, wd, gpost, *, tm, tf):
    m, d = x.shape
    dff = wg.shape[1]
    return pl.pallas_call(
        _ffn_kernel,
        grid=(m // tm, dff // tf),
        in_specs=[
            pl.BlockSpec((tm, d), lambda i, f: (i, 0)),
            pl.BlockSpec((1, d), lambda i, f: (0, 0)),
            pl.BlockSpec((d, tf), lambda i, f: (0, f)),
            pl.BlockSpec((d, tf), lambda i, f: (0, f)),
            pl.BlockSpec((tf, d), lambda i, f: (f, 0)),
            pl.BlockSpec((1, d), lambda i, f: (0, 0)),
        ],
        out_specs=pl.BlockSpec((tm, d), lambda i, f: (i, 0)),
        out_shape=jax.ShapeDtypeStruct((m, d), F32),
        scratch_shapes=[pltpu.VMEM((tm, d), BF16)],
        compiler_params=_params(2),
        name="ffn",
    )(x, gpre, wg, wu, wd, gpost)


def _lru_gates(xc, wa, ba, wx, bx, sp):
    xcb = xc.astype(BF16)
    gate_a = _sigmoid(jnp.dot(xcb, wa, preferred_element_type=F32) + ba)
    gate_x = _sigmoid(jnp.dot(xcb, wx, preferred_element_type=F32) + bx)
    log_a = (-LRU_C * gate_a) * sp
    a = jnp.exp(log_a)
    b = jnp.sqrt(-jnp.tanh(log_a) * (a * a + 1.0)) * (gate_x * xc)
    return a, b


def _lru_seq_kernel(xr_ref, yg_ref, conv0_ref, h0_ref, cw_ref, cb_ref, wa_ref, ba_ref, wx_ref, bx_ref, ap_ref,
                    o_ref, hlast_ref, xe_ref, h_ref, a_ref, b_ref, *, tt):
    ti = pl.program_id(1)
    pad = SUBLANES

    @pl.when(ti == 0)
    def _():
        xe_ref[0:pad, :] = conv0_ref[0]
        h_ref[...] = h0_ref[0]

    xe_ref[pad:pad + tt, :] = xr_ref[...]
    sp = _softplus(-ap_ref[...])
    rowmod = lax.broadcasted_iota(jnp.int32, (tt, LRU_BLOCK), 0) & (SUBLANES - 1)

    for n in range(LRU_BLOCKS):
        cs = slice(n * LRU_BLOCK, (n + 1) * LRU_BLOCK)
        xc = cb_ref[:, cs] + xe_ref[pad - 3:pad - 3 + tt, cs] * cw_ref[0:1, cs]
        for tap in range(1, CONV_WIDTH):
            xc = xc + xe_ref[pad - 3 + tap:pad - 3 + tap + tt, cs] * cw_ref[tap:tap + 1, cs]
        a, b = _lru_gates(xc, wa_ref[n], ba_ref[n], wx_ref[n], bx_ref[n], sp[:, cs])
        d = 1
        while d < SUBLANES:
            a_sh = pltpu.roll(a, d, axis=0)
            b_sh = pltpu.roll(b, d, axis=0)
            valid = rowmod >= d
            b = jnp.where(valid, a * b_sh + b, b)
            a = jnp.where(valid, a * a_sh, a)
            d *= 2
        a_ref[:, cs] = a
        b_ref[:, cs] = b

    def group(gi, carry):
        r = pl.multiple_of(gi * SUBLANES, SUBLANES)
        h = a_ref[pl.ds(r, SUBLANES), :] * carry + b_ref[pl.ds(r, SUBLANES), :]
        b_ref[pl.ds(r, SUBLANES), :] = h
        return h[SUBLANES - 1:SUBLANES, :]

    carry = lax.fori_loop(0, tt // SUBLANES, group, h_ref[...], unroll=4)
    h_ref[...] = carry
    hlast_ref[0] = carry
    o_ref[...] = (b_ref[...] * yg_ref[...]).astype(o_ref.dtype)
    xe_ref[0:pad, :] = xe_ref[tt:tt + pad, :]


def _lru_seq(proj, conv0, h0, cw, cb, wa, ba, wx, bx, ap, *, batch, seq, tt):
    w = LRU_WIDTH
    nt = seq // tt
    const2 = lambda b, t: (0, 0)
    const3 = lambda b, t: (0, 0, 0)
    return pl.pallas_call(
        functools.partial(_lru_seq_kernel, tt=tt),
        grid=(batch, nt),
        in_specs=[
            pl.BlockSpec((tt, w), lambda b, t: (b * nt + t, 1)),
            pl.BlockSpec((tt, w), lambda b, t: (b * nt + t, 0)),
            pl.BlockSpec((1, SUBLANES, w), lambda b, t: (b, 0, 0)),
            pl.BlockSpec((1, 1, w), lambda b, t: (b, 0, 0)),
            pl.BlockSpec((CONV_WIDTH, w), const2),
            pl.BlockSpec((1, w), const2),
            pl.BlockSpec((LRU_BLOCKS, LRU_BLOCK, LRU_BLOCK), const3),
            pl.BlockSpec((LRU_BLOCKS, 1, LRU_BLOCK), const3),
            pl.BlockSpec((LRU_BLOCKS, LRU_BLOCK, LRU_BLOCK), const3),
            pl.BlockSpec((LRU_BLOCKS, 1, LRU_BLOCK), const3),
            pl.BlockSpec((1, w), const2),
        ],
        out_specs=[
            pl.BlockSpec((tt, w), lambda b, t: (b * nt + t, 0)),
            pl.BlockSpec((1, 1, w), lambda b, t: (b, 0, 0)),
        ],
        out_shape=[
            jax.ShapeDtypeStruct((batch * seq, w), BF16),
            jax.ShapeDtypeStruct((batch, 1, w), F32),
        ],
        scratch_shapes=[
            pltpu.VMEM((tt + SUBLANES, w), F32),
            pltpu.VMEM((1, w), F32),
            pltpu.VMEM((tt, w), F32),
            pltpu.VMEM((tt, w), F32),
        ],
        compiler_params=_params(2),
        name="lru_seq",
    )(proj, proj, conv0, h0, cw, cb, wa, ba, wx, bx, ap)


def _lru_step_kernel(xr_ref, yg_ref, conv_ref, h0_ref, cw_ref, cb_ref, wa_ref, ba_ref, wx_ref, bx_ref, ap_ref,
                     o_ref, hnew_ref):
    sp = _softplus(-ap_ref[...])
    for n in range(LRU_BLOCKS):
        cs = slice(n * LRU_BLOCK, (n + 1) * LRU_BLOCK)
        xc = cb_ref[:, cs] + conv_ref[0, :, cs] * cw_ref[0:1, cs]
        for tap in range(1, CONV_WIDTH - 1):
            xc = xc + conv_ref[tap, :, cs] * cw_ref[tap:tap + 1, cs]
        xc = xc + xr_ref[:, cs] * cw_ref[CONV_WIDTH - 1:CONV_WIDTH, cs]
        a, b = _lru_gates(xc, wa_ref[n], ba_ref[n], wx_ref[n], bx_ref[n], sp[:, cs])
        h = a * h0_ref[:, cs] + b
        hnew_ref[:, cs] = h
        o_ref[:, cs] = h * yg_ref[:, cs]


def _lru_step(proj, conv_t, h0, cw, cb, wa, ba, wx, bx, ap):
    m = proj.shape[0]
    w = LRU_WIDTH
    c2 = lambda i: (0, 0)
    c3 = lambda i: (0, 0, 0)
    return pl.pallas_call(
        _lru_step_kernel,
        grid=(1,),
        in_specs=[
            pl.BlockSpec((m, w), lambda i: (0, 1)),
            pl.BlockSpec((m, w), lambda i: (0, 0)),
            pl.BlockSpec((CONV_WIDTH - 1, m, w), c3),
            pl.BlockSpec((m, w), c2),
            pl.BlockSpec((CONV_WIDTH, w), c2),
            pl.BlockSpec((1, w), c2),
            pl.BlockSpec((LRU_BLOCKS, LRU_BLOCK, LRU_BLOCK), c3),
            pl.BlockSpec((LRU_BLOCKS, 1, LRU_BLOCK), c3),
            pl.BlockSpec((LRU_BLOCKS, LRU_BLOCK, LRU_BLOCK), c3),
            pl.BlockSpec((LRU_BLOCKS, 1, LRU_BLOCK), c3),
            pl.BlockSpec((1, w), c2),
        ],
        out_specs=[pl.BlockSpec((m, w), c2), pl.BlockSpec((m, w), c2)],
        out_shape=[jax.ShapeDtypeStruct((m, w), F32), jax.ShapeDtypeStruct((m, w), F32)],
        compiler_params=_params(1),
        name="lru_step",
    )(proj, proj, conv_t, h0, cw, cb, wa, ba, wx, bx, ap)


def _lambda(lam_ref, lam_init):
    s1 = jnp.sum(lam_ref[0:1, :] * lam_ref[1:2, :], axis=-1, keepdims=True)
    s2 = jnp.sum(lam_ref[2:3, :] * lam_ref[3:4, :], axis=-1, keepdims=True)
    return jnp.exp(s1) - jnp.exp(s2) + lam_init


def _subln(o, g, lam_init):
    return o * _rms_scale(o, SUBLN_EPS) * g * (1.0 - lam_init)


def _nt_dot(a, b):
    return lax.dot_general(a, b, (((1,), (1,)), ((), ())), preferred_element_type=F32)


def _attn_prompt_kernel(lam_ref, g_ref, q_ref, k_ref, v_ref, o_ref, m_ref, l_ref, acc_ref, *, tq, tk, lam_init):
    qi = pl.program_id(2)
    m_ref[...] = jnp.full(m_ref.shape, -jnp.inf, F32)
    l_ref[...] = jnp.zeros(l_ref.shape, F32)
    acc_ref[...] = jnp.zeros(acc_ref.shape, F32)
    chunks_per_q = tq // tk

    def chunk(c, masked):
        r = pl.multiple_of(c * tk, tk)
        kc = k_ref[pl.ds(r, tk), :].astype(BF16)
        vc = v_ref[pl.ds(r, tk), :].astype(BF16)
        if masked:
            qpos = qi * tq + lax.broadcasted_iota(jnp.int32, (tq, tk), 0)
            kpos = c * tk + lax.broadcasted_iota(jnp.int32, (tq, tk), 1)
            keep = kpos <= qpos
        for s in range(2):
            hs = slice(s * HEAD_DIM, (s + 1) * HEAD_DIM)
            sc = _nt_dot(q_ref[:, hs], kc[:, hs])
            if masked:
                sc = jnp.where(keep, sc, -jnp.inf)
            m_old = m_ref[s]
            m_new = jnp.maximum(m_old, jnp.max(sc, axis=-1, keepdims=True))
            alpha = jnp.exp(m_old - m_new)
            p = jnp.exp(sc - m_new)
            l_ref[s] = alpha * l_ref[s] + jnp.sum(p, axis=-1, keepdims=True)
            acc_ref[s] = alpha * acc_ref[s] + jnp.dot(p.astype(BF16), vc, preferred_element_type=F32)
            m_ref[s] = m_new

    def full_chunk(c, carry):
        chunk(c, False)
        return carry

    lax.fori_loop(0, qi * chunks_per_q, full_chunk, 0)
    for dc in range(chunks_per_q):
        chunk(qi * chunks_per_q + dc, True)

    lam = _lambda(lam_ref, lam_init)
    o = acc_ref[0] / l_ref[0] - lam * (acc_ref[1] / l_ref[1])
    o_ref[...] = _subln(o, g_ref[...], lam_init).astype(o_ref.dtype)


def _attn_prompt(q, k, v, lam4, g, *, batch, seq, tq, tk, lam_init):
    hw = 2 * HEAD_DIM
    nq = seq // tq
    return pl.pallas_call(
        functools.partial(_attn_prompt_kernel, tq=tq, tk=tk, lam_init=lam_init),
        grid=(batch, N_HEADS, nq),
        in_specs=[
            pl.BlockSpec((4, HEAD_DIM), lambda b, h, i: (0, 0)),
            pl.BlockSpec((1, hw), lambda b, h, i: (0, 0)),
            pl.BlockSpec((tq, hw), lambda b, h, i: (b * nq + i, h)),
            pl.BlockSpec((seq, hw), lambda b, h, i: (b, h)),
            pl.BlockSpec((seq, hw), lambda b, h, i: (b, h)),
        ],
        out_specs=pl.BlockSpec((tq, hw), lambda b, h, i: (b * nq + i, h)),
        out_shape=jax.ShapeDtypeStruct((batch * seq, D_MODEL), BF16),
        scratch_shapes=[
            pltpu.VMEM((2, tq, 1), F32),
            pltpu.VMEM((2, tq, 1), F32),
            pltpu.VMEM((2, tq, hw), F32),
        ],
        compiler_params=_params(3),
        name="attn_prompt",
    )(lam4, g, q, k, v)


def _attn_decode_kernel(pt_ref, lam_ref, g_ref, q_ref, knew_ref, vnew_ref, *rest, pages_per_step, page, lam_init):
    del pt_ref
    npg = pages_per_step
    k_refs = rest[:npg]
    v_refs = rest[npg:2 * npg]
    o_ref, m_ref, l_ref, acc_ref = rest[2 * npg:]
    step = pl.program_id(1)
    n_maps = 2 * N_HEADS
    cols = page * N_HEADS

    @pl.when(step == 0)
    def _():
        m_ref[...] = jnp.full(m_ref.shape, -jnp.inf, F32)
        l_ref[...] = jnp.zeros(l_ref.shape, F32)
        acc_ref[...] = jnp.zeros(acc_ref.shape, F32)

    q = q_ref[0]
    qb = q.astype(BF16)
    row = lax.broadcasted_iota(jnp.int32, (n_maps, cols), 0)
    col = lax.broadcasted_iota(jnp.int32, (n_maps, cols), 1)
    first_map = row < N_HEADS
    own = (col & (N_HEADS - 1)) == (row & (N_HEADS - 1))

    scores = []
    for j in range(npg):
        k_even = k_refs[j][:, pl.ds(0, N_HEADS, stride=2), :].reshape(cols, HEAD_DIM).astype(BF16)
        k_odd = k_refs[j][:, pl.ds(1, N_HEADS, stride=2), :].reshape(cols, HEAD_DIM).astype(BF16)
        sc = jnp.where(first_map, _nt_dot(qb, k_even), _nt_dot(qb, k_odd))
        scores.append(jnp.where(own, sc, -jnp.inf))
    sc = jnp.concatenate(scores, axis=1) if npg > 1 else scores[0]
    m_old = m_ref[...]
    m_new = jnp.maximum(m_old, jnp.max(sc, axis=-1, keepdims=True))
    alpha = jnp.exp(m_old - m_new)
    p = jnp.exp(sc - m_new)
    l_ref[...] = alpha * l_ref[...] + jnp.sum(p, axis=-1, keepdims=True)
    pb = p.astype(BF16)
    pv = None
    for j in range(npg):
        vj = v_refs[j][...].reshape(cols, 2 * HEAD_DIM).astype(BF16)
        part = jnp.dot(pb[:, j * cols:(j + 1) * cols], vj, preferred_element_type=F32)
        pv = part if pv is None else pv + part
    acc_ref[...] = alpha * acc_ref[...] + pv
    m_ref[...] = m_new

    @pl.when(step == pl.num_programs(1) - 1)
    def _():
        s_new = jnp.sum(q * knew_ref[0], axis=-1, keepdims=True)
        m_old2 = m_ref[...]
        m_fin = jnp.maximum(m_old2, s_new)
        alpha2 = jnp.exp(m_old2 - m_fin)
        p_new = jnp.exp(s_new - m_fin)
        l_fin = alpha2 * l_ref[...] + p_new
        v_new = vnew_ref[0]
        acc = (alpha2 * acc_ref[...] + p_new * jnp.concatenate([v_new, v_new], axis=0)) / l_fin
        lam = _lambda(lam_ref, lam_init)
        o = acc[0:N_HEADS, :] - lam * acc[N_HEADS:n_maps, :]
        o_ref[0] = _subln(o, g_ref[...], lam_init)


def _attn_decode(q, k_new, v_new, cache_k, cache_v, layer, page_table, lam4, g, *, pages_per_step, lam_init):
    rows = q.shape[0]
    n_pages = page_table.shape[1]
    page = cache_k.shape[2]
    npg = pages_per_step
    hw = 2 * HEAD_DIM
    n_maps = 2 * N_HEADS

    def page_spec(j, heads, width):
        return pl.BlockSpec((None, None, page, heads, width),
                            lambda b, s, pt: (layer, pt[b, s * npg + j], 0, 0, 0))

    qk_spec = pl.BlockSpec((1, n_maps, HEAD_DIM), lambda b, s, pt: (b, 0, 0))
    v_spec = pl.BlockSpec((1, N_HEADS, hw), lambda b, s, pt: (b, 0, 0))
    grid_spec = pltpu.PrefetchScalarGridSpec(
        num_scalar_prefetch=1,
        grid=(rows, n_pages // npg),
        in_specs=[
            pl.BlockSpec((4, HEAD_DIM), lambda b, s, pt: (0, 0)),
            pl.BlockSpec((1, hw), lambda b, s, pt: (0, 0)),
            qk_spec, qk_spec, v_spec,
        ] + [page_spec(j, n_maps, HEAD_DIM) for j in range(npg)] + [page_spec(j, N_HEADS, hw) for j in range(npg)],
        out_specs=v_spec,
        scratch_shapes=[
            pltpu.VMEM((n_maps, 1), F32),
            pltpu.VMEM((n_maps, 1), F32),
            pltpu.VMEM((n_maps, hw), F32),
        ],
    )
    return pl.pallas_call(
        functools.partial(_attn_decode_kernel, pages_per_step=npg, page=page, lam_init=lam_init),
        grid_spec=grid_spec,
        out_shape=jax.ShapeDtypeStruct((rows, N_HEADS, hw), F32),
        compiler_params=_params(2),
        name="attn_decode",
    )(page_table, lam4, g, q, k_new, v_new, *([cache_k] * npg), *([cache_v] * npg))


def _rope_tables(pos):
    half = HEAD_DIM // 2
    inv_freq = jnp.power(ROPE_THETA, -jnp.arange(half, dtype=F32) * (2.0 / HEAD_DIM))
    ang = pos.astype(F32)[:, None] * inv_freq[None, :]
    cos = jnp.cos(ang)
    sin = jnp.sin(ang)
    return jnp.concatenate([cos, cos], axis=-1), jnp.concatenate([-sin, sin], axis=-1)


def kernel(x_prompt, x_sample, state_conv, state_h, cache_k, cache_v, page_table, norm_mix_pre, norm_mix_post, norm_ffn_pre, norm_ffn_post, lru_w_in, lru_b_in, lru_conv_w, lru_conv_b, lru_w_gate_a, lru_b_gate_a, lru_w_gate_x, lru_b_gate_x, lru_a_param, lru_w_out, attn_w_qkv, attn_lambda_q1, attn_lambda_k1, attn_lambda_q2, attn_lambda_k2, attn_subln, attn_w_o, ffn_w_gate, ffn_w_up, ffn_w_down):
    bp, seq, d = x_prompt.shape
    bd, t_dec, _ = x_sample.shape
    assert d == D_MODEL and t_dec == 1
    depth = norm_mix_pre.shape[0]
    n_pages = page_table.shape[1]
    page = cache_k.shape[2]
    past = n_pages * page
    mp = bp * seq

    xp = x_prompt.reshape(mp, d)
    xs = x_sample.reshape(bd, d)
    row = lambda v: v.reshape(1, -1)

    conv_p, h_p, k_p, v_p = [], [], [], []
    conv_s, h_s, k_s, v_s = [], [], [], []
    for i in range(depth):
        j = i // N_MIXERS
        g_pre = row(norm_mix_pre[i])
        g_post = row(norm_mix_post[i])
        if i % N_MIXERS == 0:
            w_in = lru_w_in[j].astype(BF16)
            b_in = row(lru_b_in[j])
            w_out = lru_w_out[j].astype(BF16)
            wa = lru_w_gate_a[j].astype(BF16)
            wx = lru_w_gate_x[j].astype(BF16)
            ba = lru_b_gate_a[j].reshape(LRU_BLOCKS, 1, LRU_BLOCK)
            bx = lru_b_gate_x[j].reshape(LRU_BLOCKS, 1, LRU_BLOCK)
            gate_prm = (lru_conv_w[j], row(lru_conv_b[j]), wa, ba, wx, bx, row(lru_a_param[j]))
            proj_p = _lru_in_proj(xp, g_pre, w_in, b_in, tm=1024, tn=512)
            hy_p, hl_p = _lru_seq(proj_p, jnp.zeros((bp, SUBLANES, LRU_WIDTH), F32), jnp.zeros((bp, 1, LRU_WIDTH), F32),
                                  *gate_prm, batch=bp, seq=seq, tt=256)
            conv_p.append(proj_p.reshape(bp, seq, 2 * LRU_WIDTH)[:, seq - (CONV_WIDTH - 1):, LRU_WIDTH:])
            h_p.append(hl_p.reshape(bp, LRU_WIDTH))
            xp = _out_proj_residual(hy_p, w_out, g_post, xp, tm=512)
            proj_s = _lru_in_proj(xs, g_pre, w_in, b_in, tm=bd, tn=512)
            hy_s, hl_s = _lru_step(proj_s, state_conv[j].transpose(1, 0, 2), state_h[j], *gate_prm)
            conv_s.append(jnp.concatenate([state_conv[j][:, 1:], proj_s[:, None, LRU_WIDTH:]], axis=1))
            h_s.append(hl_s)
            xs = _out_proj_residual(hy_s, w_out, g_post, xs, tm=bd)
        else:
            lam_init = 0.8 - 0.6 * math.exp(-0.3 * i)
            w_qkv = attn_w_qkv[j].astype(BF16)
            w_o = attn_w_o[j].astype(BF16)
            lam4 = jnp.stack([attn_lambda_q1[j], attn_lambda_k1[j], attn_lambda_q2[j], attn_lambda_k2[j]])
            g_sub = row(attn_subln[j])
            cos_p, sin_p = _rope_tables(jnp.arange(seq))
            q, k, v = _qkv_proj(xp, g_pre, w_qkv, cos_p, sin_p, tm=1024, tn=512)
            o = _attn_prompt(q, k, v, lam4, g_sub, batch=bp, seq=seq, tq=512, tk=512, lam_init=lam_init)
            k_p.append(k.reshape(bp, seq, 2 * N_HEADS, HEAD_DIM))
            v_p.append(v.reshape(bp, seq, N_HEADS, 2 * HEAD_DIM))
            xp = _out_proj_residual(o, w_o, g_post, xp, tm=512)
            cos_s, sin_s = _rope_tables(jnp.full((bd,), past))
            qs, ks_new, vs_new = _qkv_proj(xs, g_pre, w_qkv, cos_s, sin_s, tm=bd, tn=512)
            map_major = lambda a: a.reshape(bd, N_HEADS, 2, HEAD_DIM).transpose(0, 2, 1, 3).reshape(bd, 2 * N_HEADS, HEAD_DIM)
            os_ = _attn_decode(map_major(qs.astype(F32)), map_major(ks_new), vs_new.reshape(bd, N_HEADS, 2 * HEAD_DIM),
                               cache_k, cache_v, j, page_table, lam4, g_sub, pages_per_step=4, lam_init=lam_init)
            k_s.append(ks_new.reshape(bd, 1, 2 * N_HEADS, HEAD_DIM))
            v_s.append(vs_new.reshape(bd, 1, N_HEADS, 2 * HEAD_DIM))
            xs = _out_proj_residual(os_.reshape(bd, d), w_o, g_post, xs, tm=bd)
        w_gate = ffn_w_gate[i].astype(BF16)
        w_up = ffn_w_up[i].astype(BF16)
        w_down = ffn_w_down[i].astype(BF16)
        gf_pre = row(norm_ffn_pre[i])
        gf_post = row(norm_ffn_post[i])
        xp = _ffn(xp, gf_pre, w_gate, w_up, w_down, gf_post, tm=512, tf=512)
        xs = _ffn(xs, gf_pre, w_gate, w_up, w_down, gf_post, tm=bd, tf=512)
    return (xp.reshape(bp, seq, d), xs.reshape(bd, 1, d), jnp.stack(conv_p), jnp.stack(h_p), jnp.stack(k_p), jnp.stack(v_p),
            jnp.stack(conv_s), jnp.stack(h_s), jnp.stack(k_s), jnp.stack(v_s))
```

```python
import functools
import math

import jax
import jax.numpy as jnp
from jax import lax
from jax.experimental import pallas as pl
from jax.experimental.pallas import tpu as pltpu

D_MODEL = 2048
LRU_WIDTH = D_MODEL
LRU_BLOCK = 256
LRU_BLOCKS = LRU_WIDTH // LRU_BLOCK
CONV_WIDTH = 4
LRU_C = 8.0
HEAD_DIM = 128
N_HEADS = D_MODEL // (2 * HEAD_DIM)
ROPE_THETA = 10000.0
NORM_EPS = 1e-6
SUBLN_EPS = 1e-5
N_MIXERS = 2
Q_SCALE = HEAD_DIM ** -0.5 * math.log2(math.e)

SUBLANES = 8
LANES = 128
VMEM_LIMIT_BYTES = 56 * 1024 * 1024

BF16 = jnp.bfloat16
F32 = jnp.float32


def _params(n_grid_dims):
    return pltpu.CompilerParams(
        dimension_semantics=("arbitrary",) * n_grid_dims,
        vmem_limit_bytes=VMEM_LIMIT_BYTES,
    )


def _rms_scale(x, eps):
    return lax.rsqrt(jnp.mean(x * x, axis=-1, keepdims=True) + eps)


def _sigmoid(x):
    return 1.0 / (1.0 + jnp.exp(-x))


def _gelu_tanh(x):
    c = math.sqrt(2.0 / math.pi)
    return 0.5 * x * (1.0 + jnp.tanh(c * (x + 0.044715 * (x * x * x))))


def _softplus(z):
    return jnp.maximum(z, 0.0) + jnp.log1p(jnp.exp(-jnp.abs(z)))


def _norm_matmul_body(x_ref, g_ref, w_ref, h_ref):
    @pl.when(pl.program_id(1) == 0)
    def _():
        x = x_ref[...]
        h_ref[...] = (x * _rms_scale(x, NORM_EPS) * g_ref[...]).astype(BF16)

    return jnp.dot(h_ref[...], w_ref[...], preferred_element_type=F32)


def _lru_in_kernel(x_ref, g_ref, w_ref, b_ref, o_ref, h_ref, *, n_gate_tiles):
    acc = _norm_matmul_body(x_ref, g_ref, w_ref, h_ref) + b_ref[...]
    o_ref[...] = jnp.where(pl.program_id(1) < n_gate_tiles, _gelu_tanh(acc), acc)


def _lru_in_proj(x, g, w, b, *, tm, tn):
    m, d = x.shape
    n = w.shape[1]
    return pl.pallas_call(
        functools.partial(_lru_in_kernel, n_gate_tiles=LRU_WIDTH // tn),
        grid=(m // tm, n // tn),
        in_specs=[
            pl.BlockSpec((tm, d), lambda i, j: (i, 0), pipeline_mode=pl.Buffered(1)),
            pl.BlockSpec((1, d), lambda i, j: (0, 0)),
            pl.BlockSpec((d, tn), lambda i, j: (0, j)),
            pl.BlockSpec((1, tn), lambda i, j: (0, j)),
        ],
        out_specs=pl.BlockSpec((tm, tn), lambda i, j: (i, j)),
        out_shape=jax.ShapeDtypeStruct((m, n), F32),
        scratch_shapes=[pltpu.VMEM((tm, d), BF16)],
        compiler_params=_params(2),
        name="lru_in_proj",
    )(x, g, w, b)


def _rope(x, cos2, sin2):
    return x * cos2 + pltpu.roll(x, HEAD_DIM // 2, axis=1) * sin2


def _qkv_kernel(x_ref, g_ref, w_ref, cos_ref, sin_ref, o_ref, h_ref, *, tiles_per_part, heads_per_tile):
    acc = _norm_matmul_body(x_ref, g_ref, w_ref, h_ref)
    j = pl.program_id(1)
    scale = jnp.where(j < tiles_per_part, Q_SCALE, 1.0)
    is_v = j >= 2 * tiles_per_part
    cos2 = cos_ref[...]
    sin2 = sin_ref[...]
    parts = []
    for hd in range(heads_per_tile):
        xh = acc[:, hd * HEAD_DIM:(hd + 1) * HEAD_DIM]
        parts.append(jnp.where(is_v, xh, _rope(xh, cos2, sin2) * scale))
    o_ref[...] = jnp.concatenate(parts, axis=1)


def _qkv_proj(x, g, w, cos2, sin2, *, tm, tn):
    m, d = x.shape
    n = w.shape[1]
    n_pos_tiles = cos2.shape[0] // tm
    return pl.pallas_call(
        functools.partial(_qkv_kernel, tiles_per_part=d // tn, heads_per_tile=tn // HEAD_DIM),
        grid=(m // tm, n // tn),
        in_specs=[
            pl.BlockSpec((tm, d), lambda i, j: (i, 0), pipeline_mode=pl.Buffered(1)),
            pl.BlockSpec((1, d), lambda i, j: (0, 0)),
            pl.BlockSpec((d, tn), lambda i, j: (0, j)),
            pl.BlockSpec((tm, HEAD_DIM), lambda i, j: (i % n_pos_tiles, 0)),
            pl.BlockSpec((tm, HEAD_DIM), lambda i, j: (i % n_pos_tiles, 0)),
        ],
        out_specs=pl.BlockSpec((tm, tn), lambda i, j: (i, j)),
        out_shape=jax.ShapeDtypeStruct((m, n), F32),
        scratch_shapes=[pltpu.VMEM((tm, d), BF16)],
        compiler_params=_params(2),
        name="qkv_proj",
    )(x, g, w, cos2, sin2)


def _out_proj_kernel(a_ref, w_ref, g_ref, x_ref, o_ref):
    y = jnp.dot(a_ref[...].astype(BF16), w_ref[...], preferred_element_type=F32)
    o_ref[...] = x_ref[...] + y * _rms_scale(y, NORM_EPS) * g_ref[...]


def _out_proj_residual(a, w, g, x, *, tm):
    m, k = a.shape
    d = w.shape[1]
    return pl.pallas_call(
        _out_proj_kernel,
        grid=(m // tm,),
        in_specs=[
            pl.BlockSpec((tm, k), lambda i: (i, 0)),
            pl.BlockSpec((k, d), lambda i: (0, 0)),
            pl.BlockSpec((1, d), lambda i: (0, 0)),
            pl.BlockSpec((tm, d), lambda i: (i, 0)),
        ],
        out_specs=pl.BlockSpec((tm, d), lambda i: (i, 0)),
        out_shape=jax.ShapeDtypeStruct((m, d), F32),
        compiler_params=_params(1),
        name="out_proj_residual",
    )(a, w, g, x)


def _ffn_kernel(x_ref, gpre_ref, wg_ref, wu_ref, wd_ref, gpost_ref, o_ref, h_ref):
    f = pl.program_id(1)

    @pl.when(f == 0)
    def _():
        x = x_ref[...]
        h_ref[...] = (x * _rms_scale(x, NORM_EPS) * gpre_ref[...]).astype(BF16)
        o_ref[...] = jnp.zeros(o_ref.shape, F32)

    h = h_ref[...]
    gate = jnp.dot(h, wg_ref[...], preferred_element_type=F32)
    up = jnp.dot(h, wu_ref[...], preferred_element_type=F32)
    act = (gate * _sigmoid(gate) * up).astype(BF16)
    o_ref[...] += jnp.dot(act, wd_ref[...], preferred_element_type=F32)

    @pl.when(f == pl.num_programs(1) - 1)
    def _():
        y = o_ref[...]
        o_ref[...] = x_ref[...] + y * _rms_scale(y, NORM_EPS) * gpost_ref[...]


def _ffn(x, gpre, wg, wu, wd, gpost, *, tm, tf):
    m, d = x.shape
    dff = wg.shape[1]
    rows_once = dict(pipeline_mode=pl.Buffered(1))
    return pl.pallas_call(
        _ffn_kernel,
        grid=(m // tm, dff // tf),
        in_specs=[
            pl.BlockSpec((tm, d), lambda i, f: (i, 0), **rows_once),
            pl.BlockSpec((1, d), lambda i, f: (0, 0)),
            pl.BlockSpec((d, tf), lambda i, f: (0, f)),
            pl.BlockSpec((d, tf), lambda i, f: (0, f)),
            pl.BlockSpec((tf, d), lambda i, f: (f, 0)),
            pl.BlockSpec((1, d), lambda i, f: (0, 0)),
        ],
        out_specs=pl.BlockSpec((tm, d), lambda i, f: (i, 0), **rows_once),
        out_shape=jax.ShapeDtypeStruct((m, d), F32),
        scratch_shapes=[pltpu.VMEM((tm, d), BF16)],
        compiler_params=_params(2),
        name="ffn",
    )(x, gpre, wg, wu, wd, gpost)


def _lru_gates(xc, wa, ba, wx, bx, sp):
    xcb = xc.astype(BF16)
    gate_a = _sigmoid(jnp.dot(xcb, wa, preferred_element_type=F32) + ba)
    gate_x = _sigmoid(jnp.dot(xcb, wx, preferred_element_type=F32) + bx)
    log_a = (-LRU_C * gate_a) * sp
    a = jnp.exp(log_a)
    b = jnp.sqrt(-jnp.tanh(log_a) * (a * a + 1.0)) * (gate_x * xc)
    return a, b


def _lru_seq_kernel(xr_ref, yg_ref, conv0_ref, h0_ref, cw_ref, cb_ref, wa_ref, ba_ref, wx_ref, bx_ref, ap_ref,
                    o_ref, hlast_ref, xe_ref, h_ref, a_ref, b_ref, *, tt):
    ti = pl.program_id(1)
    pad = SUBLANES

    @pl.when(ti == 0)
    def _():
        xe_ref[0:pad, :] = conv0_ref[0]
        h_ref[...] = h0_ref[0]

    xe_ref[pad:pad + tt, :] = xr_ref[...]
    sp = _softplus(-ap_ref[...])
    rowmod = lax.broadcasted_iota(jnp.int32, (tt // SUBLANES, SUBLANES, LRU_BLOCK), 1)

    for n in range(LRU_BLOCKS):
        cs = slice(n * LRU_BLOCK, (n + 1) * LRU_BLOCK)
        xc = cb_ref[:, cs] + xe_ref[pad - 3:pad - 3 + tt, cs] * cw_ref[0:1, cs]
        for tap in range(1, CONV_WIDTH):
            xc = xc + xe_ref[pad - 3 + tap:pad - 3 + tap + tt, cs] * cw_ref[tap:tap + 1, cs]
        a, b = _lru_gates(xc, wa_ref[n], ba_ref[n], wx_ref[n], bx_ref[n], sp[:, cs])
        a = a.reshape(tt // SUBLANES, SUBLANES, LRU_BLOCK)
        b = b.reshape(tt // SUBLANES, SUBLANES, LRU_BLOCK)
        d = 1
        while d < SUBLANES:
            valid = rowmod >= d
            a_sh = jnp.where(valid, pltpu.roll(a, d, axis=1), 1.0)
            b_sh = jnp.where(valid, pltpu.roll(b, d, axis=1), 0.0)
            b = a * b_sh + b
            a = a * a_sh
            d *= 2
        a_ref[:, cs] = a.reshape(tt, LRU_BLOCK)
        b_ref[:, cs] = b.reshape(tt, LRU_BLOCK)

    def group(gi, carry):
        r = pl.multiple_of(gi * SUBLANES, SUBLANES)
        h = a_ref[pl.ds(r, SUBLANES), :] * carry + b_ref[pl.ds(r, SUBLANES), :]
        b_ref[pl.ds(r, SUBLANES), :] = h
        return h[SUBLANES - 1:SUBLANES, :]

    carry = lax.fori_loop(0, tt // SUBLANES, group, h_ref[...], unroll=4)
    h_ref[...] = carry
    hlast_ref[0] = carry
    o_ref[...] = (b_ref[...] * yg_ref[...]).astype(o_ref.dtype)
    xe_ref[0:pad, :] = xe_ref[tt:tt + pad, :]


def _lru_seq(proj, conv0, h0, cw, cb, wa, ba, wx, bx, ap, *, batch, seq, tt):
    w = LRU_WIDTH
    nt = seq // tt
    const2 = lambda b, t: (0, 0)
    const3 = lambda b, t: (0, 0, 0)
    return pl.pallas_call(
        functools.partial(_lru_seq_kernel, tt=tt),
        grid=(batch, nt),
        in_specs=[
            pl.BlockSpec((tt, w), lambda b, t: (b * nt + t, 1)),
            pl.BlockSpec((tt, w), lambda b, t: (b * nt + t, 0)),
            pl.BlockSpec((1, SUBLANES, w), lambda b, t: (b, 0, 0)),
            pl.BlockSpec((1, 1, w), lambda b, t: (b, 0, 0)),
            pl.BlockSpec((CONV_WIDTH, w), const2),
            pl.BlockSpec((1, w), const2),
            pl.BlockSpec((LRU_BLOCKS, LRU_BLOCK, LRU_BLOCK), const3),
            pl.BlockSpec((LRU_BLOCKS, 1, LRU_BLOCK), const3),
            pl.BlockSpec((LRU_BLOCKS, LRU_BLOCK, LRU_BLOCK), const3),
            pl.BlockSpec((LRU_BLOCKS, 1, LRU_BLOCK), const3),
            pl.BlockSpec((1, w), const2),
        ],
        out_specs=[
            pl.BlockSpec((tt, w), lambda b, t: (b * nt + t, 0)),
            pl.BlockSpec((1, 1, w), lambda b, t: (b, 0, 0)),
        ],
        out_shape=[
            jax.ShapeDtypeStruct((batch * seq, w), BF16),
            jax.ShapeDtypeStruct((batch, 1, w), F32),
        ],
        scratch_shapes=[
            pltpu.VMEM((tt + SUBLANES, w), F32),
            pltpu.VMEM((1, w), F32),
            pltpu.VMEM((tt, w), F32),
            pltpu.VMEM((tt, w), F32),
        ],
        compiler_params=_params(2),
        name="lru_seq",
    )(proj, proj, conv0, h0, cw, cb, wa, ba, wx, bx, ap)


def _lru_step_kernel(xr_ref, yg_ref, conv_ref, h0_ref, cw_ref, cb_ref, wa_ref, ba_ref, wx_ref, bx_ref, ap_ref,
                     o_ref, hnew_ref):
    sp = _softplus(-ap_ref[...])
    for n in range(LRU_BLOCKS):
        cs = slice(n * LRU_BLOCK, (n + 1) * LRU_BLOCK)
        xc = cb_ref[:, cs] + conv_ref[0, :, cs] * cw_ref[0:1, cs]
        for tap in range(1, CONV_WIDTH - 1):
            xc = xc + conv_ref[tap, :, cs] * cw_ref[tap:tap + 1, cs]
        xc = xc + xr_ref[:, cs] * cw_ref[CONV_WIDTH - 1:CONV_WIDTH, cs]
        a, b = _lru_gates(xc, wa_ref[n], ba_ref[n], wx_ref[n], bx_ref[n], sp[:, cs])
        h = a * h0_ref[:, cs] + b
        hnew_ref[:, cs] = h
        o_ref[:, cs] = h * yg_ref[:, cs]


def _lru_step(proj, conv_t, h0, cw, cb, wa, ba, wx, bx, ap):
    m = proj.shape[0]
    w = LRU_WIDTH
    c2 = lambda i: (0, 0)
    c3 = lambda i: (0, 0, 0)
    return pl.pallas_call(
        _lru_step_kernel,
        grid=(1,),
        in_specs=[
            pl.BlockSpec((m, w), lambda i: (0, 1)),
            pl.BlockSpec((m, w), lambda i: (0, 0)),
            pl.BlockSpec((CONV_WIDTH - 1, m, w), c3),
            pl.BlockSpec((m, w), c2),
            pl.BlockSpec((CONV_WIDTH, w), c2),
            pl.BlockSpec((1, w), c2),
            pl.BlockSpec((LRU_BLOCKS, LRU_BLOCK, LRU_BLOCK), c3),
            pl.BlockSpec((LRU_BLOCKS, 1, LRU_BLOCK), c3),
            pl.BlockSpec((LRU_BLOCKS, LRU_BLOCK, LRU_BLOCK), c3),
            pl.BlockSpec((LRU_BLOCKS, 1, LRU_BLOCK), c3),
            pl.BlockSpec((1, w), c2),
        ],
        out_specs=[pl.BlockSpec((m, w), c2), pl.BlockSpec((m, w), c2)],
        out_shape=[jax.ShapeDtypeStruct((m, w), F32), jax.ShapeDtypeStruct((m, w), F32)],
        compiler_params=_params(1),
        name="lru_step",
    )(proj, proj, conv_t, h0, cw, cb, wa, ba, wx, bx, ap)


def _lambda(lam_ref, lam_init):
    s1 = jnp.sum(lam_ref[0:1, :] * lam_ref[1:2, :], axis=-1, keepdims=True)
    s2 = jnp.sum(lam_ref[2:3, :] * lam_ref[3:4, :], axis=-1, keepdims=True)
    return jnp.exp(s1) - jnp.exp(s2) + lam_init


def _subln(o, g, lam_init):
    return o * _rms_scale(o, SUBLN_EPS) * g * (1.0 - lam_init)


def _nt_dot(a, b):
    return lax.dot_general(a, b, (((1,), (1,)), ((), ())), preferred_element_type=F32)


def _attn_prompt_kernel(lam_ref, g_ref, q_ref, k_ref, v_ref, o_ref, qb_ref, kb_ref, vb_ref, m_ref, l_ref, acc_ref,
                        *, tile, lam_init):
    qi = pl.program_id(2)
    maps = range(2)
    hs = [slice(s * HEAD_DIM, (s + 1) * HEAD_DIM) for s in maps]
    groups = tile // LANES

    @pl.when(qi == 0)
    def _():
        kb_ref[...] = k_ref[...].astype(BF16)
        vb_ref[...] = v_ref[...].astype(BF16)

    qb_ref[...] = q_ref[...].astype(BF16)

    def scores(c, diagonal):
        kc = kb_ref[pl.ds(pl.multiple_of(c * tile, tile), tile), :]
        sc = [_nt_dot(qb_ref[:, hs[s]], kc[:, hs[s]]) for s in maps]
        if diagonal:
            keep = (lax.broadcasted_iota(jnp.int32, (tile, tile), 1)
                    <= lax.broadcasted_iota(jnp.int32, (tile, tile), 0))
            sc = [jnp.where(keep, x, -jnp.inf) for x in sc]
        return sc

    def lane_groups(x):
        return [x[:, j * LANES:(j + 1) * LANES] for j in range(groups)]

    def max_pass(c, diagonal):
        sc = scores(c, diagonal)
        m_old = [m_ref[s] for s in maps]
        for s in maps:
            m_ref[s] = functools.reduce(jnp.maximum, lane_groups(sc[s]), m_old[s])

    def sum_pass(c, diagonal):
        sc = scores(c, diagonal)
        vc = vb_ref[pl.ds(pl.multiple_of(c * tile, tile), tile), :]
        m_row = [m_ref[s] for s in maps]
        l_old = [l_ref[s] for s in maps]
        acc_old = [acc_ref[s] for s in maps]
        p = [[jnp.exp2(x - m_row[s]) for x in lane_groups(sc[s])] for s in maps]
        for s in maps:
            l_ref[s] = functools.reduce(jnp.add, p[s], l_old[s])
            pb = jnp.concatenate(p[s], axis=1).astype(BF16)
            acc_ref[s] = acc_old[s] + jnp.dot(pb, vc, preferred_element_type=F32)

    def loop(body):
        def step(c, carry):
            body(c, False)
            return carry
        lax.fori_loop(0, qi, step, 0)
        body(qi, True)

    m_ref[...] = jnp.full(m_ref.shape, -jnp.inf, F32)
    loop(max_pass)
    for s in maps:
        m_ref[s] = jnp.broadcast_to(jnp.max(m_ref[s], axis=-1, keepdims=True), (tile, LANES))
    l_ref[...] = jnp.zeros(l_ref.shape, F32)
    acc_ref[...] = jnp.zeros(acc_ref.shape, F32)
    loop(sum_pass)

    lam = _lambda(lam_ref, lam_init)
    inv = [1.0 / jnp.sum(l_ref[s], axis=-1, keepdims=True) for s in maps]
    o = acc_ref[0] * inv[0] - lam * (acc_ref[1] * inv[1])
    o_ref[...] = _subln(o, g_ref[...], lam_init).astype(o_ref.dtype)


def _attn_prompt(qkv, lam4, g, *, batch, seq, tile, lam_init):
    hw = 2 * HEAD_DIM
    nq = seq // tile
    part = D_MODEL // hw
    return pl.pallas_call(
        functools.partial(_attn_prompt_kernel, tile=tile, lam_init=lam_init),
        grid=(batch, N_HEADS, nq),
        in_specs=[
            pl.BlockSpec((4, HEAD_DIM), lambda b, h, i: (0, 0)),
            pl.BlockSpec((1, hw), lambda b, h, i: (0, 0)),
            pl.BlockSpec((tile, hw), lambda b, h, i: (b * nq + i, h)),
            pl.BlockSpec((seq, hw), lambda b, h, i: (b, part + h)),
            pl.BlockSpec((seq, hw), lambda b, h, i: (b, 2 * part + h)),
        ],
        out_specs=pl.BlockSpec((tile, hw), lambda b, h, i: (b * nq + i, h)),
        out_shape=jax.ShapeDtypeStruct((batch * seq, D_MODEL), BF16),
        scratch_shapes=[
            pltpu.VMEM((tile, hw), BF16),
            pltpu.VMEM((seq, hw), BF16),
            pltpu.VMEM((seq, hw), BF16),
            pltpu.VMEM((2, tile, LANES), F32),
            pltpu.VMEM((2, tile, LANES), F32),
            pltpu.VMEM((2, tile, hw), F32),
        ],
        compiler_params=_params(3),
        name="attn_prompt",
    )(lam4, g, qkv, qkv, qkv)


def _attn_decode_kernel(pt_ref, lam_ref, g_ref, q_ref, knew_ref, vnew_ref, *rest, pages_per_step, page, lam_init):
    del pt_ref
    npg = pages_per_step
    k_refs = rest[:npg]
    v_refs = rest[npg:2 * npg]
    o_ref, m_ref, l_ref, acc_ref = rest[2 * npg:]
    step = pl.program_id(1)
    n_maps = 2 * N_HEADS
    cols = page * N_HEADS

    @pl.when(step == 0)
    def _():
        m_ref[...] = jnp.full(m_ref.shape, -jnp.inf, F32)
        l_ref[...] = jnp.zeros(l_ref.shape, F32)
        acc_ref[...] = jnp.zeros(acc_ref.shape, F32)

    q = q_ref[0]
    qb = q.astype(BF16)
    row = lax.broadcasted_iota(jnp.int32, (n_maps, cols), 0)
    col = lax.broadcasted_iota(jnp.int32, (n_maps, cols), 1)
    first_map = row < N_HEADS
    own = (col & (N_HEADS - 1)) == (row & (N_HEADS - 1))

    scores = []
    for j in range(npg):
        k_even = k_refs[j][:, pl.ds(0, N_HEADS, stride=2), :].reshape(cols, HEAD_DIM).astype(BF16)
        k_odd = k_refs[j][:, pl.ds(1, N_HEADS, stride=2), :].reshape(cols, HEAD_DIM).astype(BF16)
        sc = jnp.where(first_map, _nt_dot(qb, k_even), _nt_dot(qb, k_odd))
        scores.append(jnp.where(own, sc, -jnp.inf))
    sc = jnp.concatenate(scores, axis=1) if npg > 1 else scores[0]
    m_old = m_ref[...]
    m_new = jnp.maximum(m_old, jnp.max(sc, axis=-1, keepdims=True))
    alpha = jnp.exp2(m_old - m_new)
    p = jnp.exp2(sc - m_new)
    l_ref[...] = alpha * l_ref[...] + jnp.sum(p, axis=-1, keepdims=True)
    pb = p.astype(BF16)
    pv = None
    for j in range(npg):
        vj = v_refs[j][...].reshape(cols, 2 * HEAD_DIM).astype(BF16)
        part = jnp.dot(pb[:, j * cols:(j + 1) * cols], vj, preferred_element_type=F32)
        pv = part if pv is None else pv + part
    acc_ref[...] = alpha * acc_ref[...] + pv
    m_ref[...] = m_new

    @pl.when(step == pl.num_programs(1) - 1)
    def _():
        s_new = jnp.sum(q * knew_ref[0], axis=-1, keepdims=True)
        m_old2 = m_ref[...]
        m_fin = jnp.maximum(m_old2, s_new)
        alpha2 = jnp.exp2(m_old2 - m_fin)
        p_new = jnp.exp2(s_new - m_fin)
        l_fin = alpha2 * l_ref[...] + p_new
        v_new = vnew_ref[0]
        acc = (alpha2 * acc_ref[...] + p_new * jnp.concatenate([v_new, v_new], axis=0)) / l_fin
        lam = _lambda(lam_ref, lam_init)
        o = acc[0:N_HEADS, :] - lam * acc[N_HEADS:n_maps, :]
        o_ref[0] = _subln(o, g_ref[...], lam_init)


def _attn_decode(q, k_new, v_new, cache_k, cache_v, layer, page_table, lam4, g, *, pages_per_step, lam_init):
    rows = q.shape[0]
    n_pages = page_table.shape[1]
    page = cache_k.shape[2]
    npg = pages_per_step
    hw = 2 * HEAD_DIM
    n_maps = 2 * N_HEADS

    def page_spec(j, heads, width):
        return pl.BlockSpec((None, None, page, heads, width),
                            lambda b, s, pt: (layer, pt[b, s * npg + j], 0, 0, 0))

    qk_spec = pl.BlockSpec((1, n_maps, HEAD_DIM), lambda b, s, pt: (b, 0, 0))
    v_spec = pl.BlockSpec((1, N_HEADS, hw), lambda b, s, pt: (b, 0, 0))
    grid_spec = pltpu.PrefetchScalarGridSpec(
        num_scalar_prefetch=1,
        grid=(rows, n_pages // npg),
        in_specs=[
            pl.BlockSpec((4, HEAD_DIM), lambda b, s, pt: (0, 0)),
            pl.BlockSpec((1, hw), lambda b, s, pt: (0, 0)),
            qk_spec, qk_spec, v_spec,
        ] + [page_spec(j, n_maps, HEAD_DIM) for j in range(npg)] + [page_spec(j, N_HEADS, hw) for j in range(npg)],
        out_specs=v_spec,
        scratch_shapes=[
            pltpu.VMEM((n_maps, 1), F32),
            pltpu.VMEM((n_maps, 1), F32),
            pltpu.VMEM((n_maps, hw), F32),
        ],
    )
    return pl.pallas_call(
        functools.partial(_attn_decode_kernel, pages_per_step=npg, page=page, lam_init=lam_init),
        grid_spec=grid_spec,
        out_shape=jax.ShapeDtypeStruct((rows, N_HEADS, hw), F32),
        compiler_params=_params(2),
        name="attn_decode",
    )(page_table, lam4, g, q, k_new, v_new, *([cache_k] * npg), *([cache_v] * npg))


def _rope_tables(pos):
    half = HEAD_DIM // 2
    inv_freq = jnp.power(ROPE_THETA, -jnp.arange(half, dtype=F32) * (2.0 / HEAD_DIM))
    ang = pos.astype(F32)[:, None] * inv_freq[None, :]
    cos = jnp.cos(ang)
    sin = jnp.sin(ang)
    return jnp.concatenate([cos, cos], axis=-1), jnp.concatenate([-sin, sin], axis=-1)


def kernel(x_prompt, x_sample, state_conv, state_h, cache_k, cache_v, page_table, norm_mix_pre, norm_mix_post, norm_ffn_pre, norm_ffn_post, lru_w_in, lru_b_in, lru_conv_w, lru_conv_b, lru_w_gate_a, lru_b_gate_a, lru_w_gate_x, lru_b_gate_x, lru_a_param, lru_w_out, attn_w_qkv, attn_lambda_q1, attn_lambda_k1, attn_lambda_q2, attn_lambda_k2, attn_subln, attn_w_o, ffn_w_gate, ffn_w_up, ffn_w_down):
    bp, seq, d = x_prompt.shape
    bd, t_dec, _ = x_sample.shape
    assert d == D_MODEL and t_dec == 1
    depth = norm_mix_pre.shape[0]
    n_pages = page_table.shape[1]
    page = cache_k.shape[2]
    past = n_pages * page
    mp = bp * seq

    xp = x_prompt.reshape(mp, d)
    xs = x_sample.reshape(bd, d)
    row = lambda v: v.reshape(1, -1)

    conv_p, h_p, k_p, v_p = [], [], [], []
    conv_s, h_s, k_s, v_s = [], [], [], []
    for i in range(depth):
        j = i // N_MIXERS
        g_pre = row(norm_mix_pre[i])
        g_post = row(norm_mix_post[i])
        if i % N_MIXERS == 0:
            w_in = lru_w_in[j].astype(BF16)
            b_in = row(lru_b_in[j])
            w_out = lru_w_out[j].astype(BF16)
            wa = lru_w_gate_a[j].astype(BF16)
            wx = lru_w_gate_x[j].astype(BF16)
            ba = lru_b_gate_a[j].reshape(LRU_BLOCKS, 1, LRU_BLOCK)
            bx = lru_b_gate_x[j].reshape(LRU_BLOCKS, 1, LRU_BLOCK)
            gate_prm = (lru_conv_w[j], row(lru_conv_b[j]), wa, ba, wx, bx, row(lru_a_param[j]))
            proj_p = _lru_in_proj(xp, g_pre, w_in, b_in, tm=1024, tn=1024)
            hy_p, hl_p = _lru_seq(proj_p, jnp.zeros((bp, SUBLANES, LRU_WIDTH), F32), jnp.zeros((bp, 1, LRU_WIDTH), F32),
                                  *gate_prm, batch=bp, seq=seq, tt=256)
            conv_p.append(proj_p.reshape(bp, seq, 2 * LRU_WIDTH)[:, seq - (CONV_WIDTH - 1):, LRU_WIDTH:])
            h_p.append(hl_p.reshape(bp, LRU_WIDTH))
            xp = _out_proj_residual(hy_p, w_out, g_post, xp, tm=512)
            proj_s = _lru_in_proj(xs, g_pre, w_in, b_in, tm=bd, tn=1024)
            hy_s, hl_s = _lru_step(proj_s, state_conv[j].transpose(1, 0, 2), state_h[j], *gate_prm)
            conv_s.append(jnp.concatenate([state_conv[j][:, 1:], proj_s[:, None, LRU_WIDTH:]], axis=1))
            h_s.append(hl_s)
            xs = _out_proj_residual(hy_s, w_out, g_post, xs, tm=bd)
        else:
            lam_init = 0.8 - 0.6 * math.exp(-0.3 * i)
            w_qkv = attn_w_qkv[j].astype(BF16)
            w_o = attn_w_o[j].astype(BF16)
            lam4 = jnp.stack([attn_lambda_q1[j], attn_lambda_k1[j], attn_lambda_q2[j], attn_lambda_k2[j]])
            g_sub = row(attn_subln[j])
            cos_p, sin_p = _rope_tables(jnp.arange(seq))
            qkv_p = _qkv_proj(xp, g_pre, w_qkv, cos_p, sin_p, tm=1024, tn=1024)
            o = _attn_prompt(qkv_p, lam4, g_sub, batch=bp, seq=seq, tile=512, lam_init=lam_init)
            k_p.append(qkv_p[:, d:2 * d].reshape(bp, seq, 2 * N_HEADS, HEAD_DIM))
            v_p.append(qkv_p[:, 2 * d:].reshape(bp, seq, N_HEADS, 2 * HEAD_DIM))
            xp = _out_proj_residual(o, w_o, g_post, xp, tm=512)
            cos_s, sin_s = _rope_tables(jnp.full((bd,), past))
            qkv_s = _qkv_proj(xs, g_pre, w_qkv, cos_s, sin_s, tm=bd, tn=1024)
            qs, ks_new, vs_new = qkv_s[:, :d], qkv_s[:, d:2 * d], qkv_s[:, 2 * d:]
            map_major = lambda a: a.reshape(bd, N_HEADS, 2, HEAD_DIM).transpose(0, 2, 1, 3).reshape(bd, 2 * N_HEADS, HEAD_DIM)
            os_ = _attn_decode(map_major(qs), map_major(ks_new), vs_new.reshape(bd, N_HEADS, 2 * HEAD_DIM),
                               cache_k, cache_v, j, page_table, lam4, g_sub, pages_per_step=8, lam_init=lam_init)
            k_s.append(ks_new.reshape(bd, 1, 2 * N_HEADS, HEAD_DIM))
            v_s.append(vs_new.reshape(bd, 1, N_HEADS, 2 * HEAD_DIM))
            xs = _out_proj_residual(os_.reshape(bd, d), w_o, g_post, xs, tm=bd)
        w_gate = ffn_w_gate[i].astype(BF16)
        w_up = ffn_w_up[i].astype(BF16)
        w_down = ffn_w_down[i].astype(BF16)
        gf_pre = row(norm_ffn_pre[i])
        gf_post = row(norm_ffn_post[i])
        xp = _ffn(xp, gf_pre, w_gate, w_up, w_down, gf_post, tm=1024, tf=512)
        xs = _ffn(xs, gf_pre, w_gate, w_up, w_down, gf_post, tm=bd, tf=512)
    return (xp.reshape(bp, seq, d), xs.reshape(bd, 1, d), jnp.stack(conv_p), jnp.stack(h_p), jnp.stack(k_p), jnp.stack(v_p),
            jnp.stack(conv_s), jnp.stack(h_s), jnp.stack(k_s), jnp.stack(v_s))
```

```python
import functools
import math

import jax
import jax.numpy as jnp
from jax import lax
from jax.experimental import pallas as pl
from jax.experimental.pallas import tpu as pltpu

D_MODEL = 2048
LRU_WIDTH = D_MODEL
LRU_BLOCK = 256
LRU_BLOCKS = LRU_WIDTH // LRU_BLOCK
CONV_WIDTH = 4
LRU_C = 8.0
HEAD_DIM = 128
N_HEADS = D_MODEL // (2 * HEAD_DIM)
ROPE_THETA = 10000.0
NORM_EPS = 1e-6
SUBLN_EPS = 1e-5
N_MIXERS = 2
Q_SCALE = HEAD_DIM ** -0.5 * math.log2(math.e)

SUBLANES = 8
LANES = 128
VMEM_LIMIT_BYTES = 56 * 1024 * 1024

BF16 = jnp.bfloat16
F32 = jnp.float32


def _params(n_grid_dims):
    return pltpu.CompilerParams(
        dimension_semantics=("arbitrary",) * n_grid_dims,
        vmem_limit_bytes=VMEM_LIMIT_BYTES,
    )


def _rms_scale(x, eps):
    return lax.rsqrt(jnp.mean(x * x, axis=-1, keepdims=True) + eps)


def _sigmoid(x):
    return 1.0 / (1.0 + jnp.exp(-x))


def _gelu_tanh(x):
    c = math.sqrt(2.0 / math.pi)
    return 0.5 * x * (1.0 + jnp.tanh(c * (x + 0.044715 * (x * x * x))))


def _softplus(z):
    return jnp.maximum(z, 0.0) + jnp.log1p(jnp.exp(-jnp.abs(z)))


def _norm_once(x_ref, g_ref, h_ref):
    @pl.when(pl.program_id(1) == 0)
    def _():
        x = x_ref[...]
        h_ref[...] = (x * _rms_scale(x, NORM_EPS) * g_ref[...]).astype(BF16)


def _lru_in_kernel(x_ref, g_ref, w_ref, b_ref, o_ref, h_ref, *, n_gate_tiles):
    _norm_once(x_ref, g_ref, h_ref)
    j = pl.program_id(1)
    proj = lambda: jnp.dot(h_ref[...], w_ref[...], preferred_element_type=F32) + b_ref[...]

    @pl.when(j < n_gate_tiles)
    def _():
        o_ref[...] = _gelu_tanh(proj())

    @pl.when(j >= n_gate_tiles)
    def _():
        o_ref[...] = proj()


def _lru_in_proj(x, g, w, b, *, tm, tn):
    m, d = x.shape
    n = w.shape[1]
    return pl.pallas_call(
        functools.partial(_lru_in_kernel, n_gate_tiles=LRU_WIDTH // tn),
        grid=(m // tm, n // tn),
        in_specs=[
            pl.BlockSpec((tm, d), lambda i, j: (i, 0), pipeline_mode=pl.Buffered(1)),
            pl.BlockSpec((1, d), lambda i, j: (0, 0)),
            pl.BlockSpec((d, tn), lambda i, j: (0, j)),
            pl.BlockSpec((1, tn), lambda i, j: (0, j)),
        ],
        out_specs=pl.BlockSpec((tm, tn), lambda i, j: (i, j)),
        out_shape=jax.ShapeDtypeStruct((m, n), F32),
        scratch_shapes=[pltpu.VMEM((tm, d), BF16)],
        compiler_params=_params(2),
        name="lru_in_proj",
    )(x, g, w, b)


def _rope(x, cos2, sin2):
    return x * cos2 + pltpu.roll(x, HEAD_DIM // 2, axis=1) * sin2


def _qkv_kernel(x_ref, g_ref, w_ref, cos_ref, sin_ref, o_ref, k4_ref, v4_ref, h_ref, *, tiles_per_part, heads_per_tile):
    _norm_once(x_ref, g_ref, h_ref)
    j = pl.program_id(1)
    tm = o_ref.shape[0]
    proj = lambda: jnp.dot(h_ref[...], w_ref[...], preferred_element_type=F32)

    def roped(acc, scale):
        cos2 = cos_ref[...]
        sin2 = sin_ref[...]
        parts = []
        for hd in range(heads_per_tile):
            r = _rope(acc[:, hd * HEAD_DIM:(hd + 1) * HEAD_DIM], cos2, sin2)
            parts.append(r if scale is None else r * scale)
        return jnp.concatenate(parts, axis=1)

    @pl.when(j < tiles_per_part)
    def _():
        o_ref[...] = roped(proj(), Q_SCALE)

    @pl.when((j >= tiles_per_part) & (j < 2 * tiles_per_part))
    def _():
        k = roped(proj(), None)
        o_ref[...] = k
        k4_ref[...] = k.reshape(tm, heads_per_tile, HEAD_DIM)

    v_heads = heads_per_tile // 2
    for t in range(tiles_per_part):
        @pl.when(j == 2 * tiles_per_part + t)
        def _(t=t):
            v = proj()
            o_ref[...] = v
            v4_ref[:, t * v_heads:(t + 1) * v_heads, :] = v.reshape(tm, v_heads, 2 * HEAD_DIM)


def _qkv_proj(x, g, w, cos2, sin2, *, tm, tn):
    m, d = x.shape
    n = w.shape[1]
    tpp = d // tn
    hpt = tn // HEAD_DIM
    n_pos_tiles = cos2.shape[0] // tm
    return pl.pallas_call(
        functools.partial(_qkv_kernel, tiles_per_part=tpp, heads_per_tile=hpt),
        grid=(m // tm, n // tn),
        in_specs=[
            pl.BlockSpec((tm, d), lambda i, j: (i, 0), pipeline_mode=pl.Buffered(1)),
            pl.BlockSpec((1, d), lambda i, j: (0, 0)),
            pl.BlockSpec((d, tn), lambda i, j: (0, j)),
            pl.BlockSpec((tm, HEAD_DIM), lambda i, j: (i % n_pos_tiles, 0)),
            pl.BlockSpec((tm, HEAD_DIM), lambda i, j: (i % n_pos_tiles, 0)),
        ],
        out_specs=[
            pl.BlockSpec((tm, tn), lambda i, j: (i, j)),
            pl.BlockSpec((tm, hpt, HEAD_DIM), lambda i, j: (i, jnp.clip(j - tpp, 0, tpp - 1), 0)),
            pl.BlockSpec((tm, N_HEADS, 2 * HEAD_DIM), lambda i, j: (i, 0, 0), pipeline_mode=pl.Buffered(1)),
        ],
        out_shape=[
            jax.ShapeDtypeStruct((m, n), F32),
            jax.ShapeDtypeStruct((m, 2 * N_HEADS, HEAD_DIM), F32),
            jax.ShapeDtypeStruct((m, N_HEADS, 2 * HEAD_DIM), F32),
        ],
        scratch_shapes=[pltpu.VMEM((tm, d), BF16)],
        compiler_params=_params(2),
        name="qkv_proj",
    )(x, g, w, cos2, sin2)


def _out_proj_kernel(a_ref, w_ref, g_ref, x_ref, o_ref):
    y = jnp.dot(a_ref[...].astype(BF16), w_ref[...], preferred_element_type=F32)
    o_ref[...] = x_ref[...] + y * _rms_scale(y, NORM_EPS) * g_ref[...]


def _out_proj_residual(a, w, g, x, *, tm):
    m, k = a.shape
    d = w.shape[1]
    return pl.pallas_call(
        _out_proj_kernel,
        grid=(m // tm,),
        in_specs=[
            pl.BlockSpec((tm, k), lambda i: (i, 0)),
            pl.BlockSpec((k, d), lambda i: (0, 0)),
            pl.BlockSpec((1, d), lambda i: (0, 0)),
            pl.BlockSpec((tm, d), lambda i: (i, 0)),
        ],
        out_specs=pl.BlockSpec((tm, d), lambda i: (i, 0)),
        out_shape=jax.ShapeDtypeStruct((m, d), F32),
        compiler_params=_params(1),
        name="out_proj_residual",
    )(a, w, g, x)


def _ffn_kernel(x_ref, gpre_ref, wg_ref, wu_ref, wd_ref, gpost_ref, o_ref, h_ref):
    f = pl.program_id(1)

    @pl.when(f == 0)
    def _():
        x = x_ref[...]
        h_ref[...] = (x * _rms_scale(x, NORM_EPS) * gpre_ref[...]).astype(BF16)
        o_ref[...] = jnp.zeros(o_ref.shape, F32)

    h = h_ref[...]
    gate = jnp.dot(h, wg_ref[...].astype(BF16), preferred_element_type=F32)
    up = jnp.dot(h, wu_ref[...].astype(BF16), preferred_element_type=F32)
    act = (gate * _sigmoid(gate) * up).astype(BF16)
    o_ref[...] += jnp.dot(act, wd_ref[...].astype(BF16), preferred_element_type=F32)

    @pl.when(f == pl.num_programs(1) - 1)
    def _():
        y = o_ref[...]
        o_ref[...] = x_ref[...] + y * _rms_scale(y, NORM_EPS) * gpost_ref[...]


def _ffn(x, gpre, wg, wu, wd, gpost, *, tm, tf):
    m, d = x.shape
    dff = wg.shape[1]
    rows_once = dict(pipeline_mode=pl.Buffered(1))
    return pl.pallas_call(
        _ffn_kernel,
        grid=(m // tm, dff // tf),
        in_specs=[
            pl.BlockSpec((tm, d), lambda i, f: (i, 0), **rows_once),
            pl.BlockSpec((1, d), lambda i, f: (0, 0)),
            pl.BlockSpec((d, tf), lambda i, f: (0, f)),
            pl.BlockSpec((d, tf), lambda i, f: (0, f)),
            pl.BlockSpec((tf, d), lambda i, f: (f, 0)),
            pl.BlockSpec((1, d), lambda i, f: (0, 0)),
        ],
        out_specs=pl.BlockSpec((tm, d), lambda i, f: (i, 0), **rows_once),
        out_shape=jax.ShapeDtypeStruct((m, d), F32),
        scratch_shapes=[pltpu.VMEM((tm, d), BF16)],
        compiler_params=_params(2),
        name="ffn",
    )(x, gpre, wg, wu, wd, gpost)


def _lru_gates(xc, wa, ba, wx, bx, sp):
    xcb = xc.astype(BF16)
    gate_a = _sigmoid(jnp.dot(xcb, wa, preferred_element_type=F32) + ba)
    gate_x = _sigmoid(jnp.dot(xcb, wx, preferred_element_type=F32) + bx)
    log_a = (-LRU_C * gate_a) * sp
    a = jnp.exp(log_a)
    b = jnp.sqrt(-jnp.tanh(log_a) * (a * a + 1.0)) * (gate_x * xc)
    return a, b


def _lru_seq_kernel(xr_ref, yg_ref, conv0_ref, h0_ref, cw_ref, cb_ref, wa_ref, ba_ref, wx_ref, bx_ref, ap_ref,
                    o_ref, hlast_ref, xe_ref, h_ref, a_ref, b_ref, *, tt):
    ti = pl.program_id(1)
    pad = SUBLANES

    @pl.when(ti == 0)
    def _():
        xe_ref[0:pad, :] = conv0_ref[0]
        h_ref[...] = h0_ref[0]

    xe_ref[pad:pad + tt, :] = xr_ref[...]
    sp = _softplus(-ap_ref[...])
    rowmod = lax.broadcasted_iota(jnp.int32, (tt // SUBLANES, SUBLANES, LRU_BLOCK), 1)

    for n in range(LRU_BLOCKS):
        cs = slice(n * LRU_BLOCK, (n + 1) * LRU_BLOCK)
        xc = cb_ref[:, cs] + xe_ref[pad - 3:pad - 3 + tt, cs] * cw_ref[0:1, cs]
        for tap in range(1, CONV_WIDTH):
            xc = xc + xe_ref[pad - 3 + tap:pad - 3 + tap + tt, cs] * cw_ref[tap:tap + 1, cs]
        a, b = _lru_gates(xc, wa_ref[n], ba_ref[n], wx_ref[n], bx_ref[n], sp[:, cs])
        a = a.reshape(tt // SUBLANES, SUBLANES, LRU_BLOCK)
        b = b.reshape(tt // SUBLANES, SUBLANES, LRU_BLOCK)
        d = 1
        while d < SUBLANES:
            valid = rowmod >= d
            a_sh = jnp.where(valid, pltpu.roll(a, d, axis=1), 1.0)
            b_sh = jnp.where(valid, pltpu.roll(b, d, axis=1), 0.0)
            b = a * b_sh + b
            a = a * a_sh
            d *= 2
        a_ref[:, cs] = a.reshape(tt, LRU_BLOCK)
        b_ref[:, cs] = b.reshape(tt, LRU_BLOCK)

    def group(gi, carry):
        r = pl.multiple_of(gi * SUBLANES, SUBLANES)
        h = a_ref[pl.ds(r, SUBLANES), :] * carry + b_ref[pl.ds(r, SUBLANES), :]
        b_ref[pl.ds(r, SUBLANES), :] = h
        return h[SUBLANES - 1:SUBLANES, :]

    carry = lax.fori_loop(0, tt // SUBLANES, group, h_ref[...], unroll=4)
    h_ref[...] = carry
    hlast_ref[0] = carry
    o_ref[...] = (b_ref[...] * yg_ref[...]).astype(o_ref.dtype)
    xe_ref[0:pad, :] = xe_ref[tt:tt + pad, :]


def _lru_seq(proj, conv0, h0, cw, cb, wa, ba, wx, bx, ap, *, batch, seq, tt):
    w = LRU_WIDTH
    nt = seq // tt
    const2 = lambda b, t: (0, 0)
    const3 = lambda b, t: (0, 0, 0)
    return pl.pallas_call(
        functools.partial(_lru_seq_kernel, tt=tt),
        grid=(batch, nt),
        in_specs=[
            pl.BlockSpec((tt, w), lambda b, t: (b * nt + t, 1)),
            pl.BlockSpec((tt, w), lambda b, t: (b * nt + t, 0)),
            pl.BlockSpec((1, SUBLANES, w), lambda b, t: (b, 0, 0)),
            pl.BlockSpec((1, 1, w), lambda b, t: (b, 0, 0)),
            pl.BlockSpec((CONV_WIDTH, w), const2),
            pl.BlockSpec((1, w), const2),
            pl.BlockSpec((LRU_BLOCKS, LRU_BLOCK, LRU_BLOCK), const3),
            pl.BlockSpec((LRU_BLOCKS, 1, LRU_BLOCK), const3),
            pl.BlockSpec((LRU_BLOCKS, LRU_BLOCK, LRU_BLOCK), const3),
            pl.BlockSpec((LRU_BLOCKS, 1, LRU_BLOCK), const3),
            pl.BlockSpec((1, w), const2),
        ],
        out_specs=[
            pl.BlockSpec((tt, w), lambda b, t: (b * nt + t, 0)),
            pl.BlockSpec((1, 1, w), lambda b, t: (b, 0, 0)),
        ],
        out_shape=[
            jax.ShapeDtypeStruct((batch * seq, w), BF16),
            jax.ShapeDtypeStruct((batch, 1, w), F32),
        ],
        scratch_shapes=[
            pltpu.VMEM((tt + SUBLANES, w), F32),
            pltpu.VMEM((1, w), F32),
            pltpu.VMEM((tt, w), F32),
            pltpu.VMEM((tt, w), F32),
        ],
        compiler_params=_params(2),
        name="lru_seq",
    )(proj, proj, conv0, h0, cw, cb, wa, ba, wx, bx, ap)


def _lru_step_kernel(xr_ref, yg_ref, conv_ref, h0_ref, cw_ref, cb_ref, wa_ref, ba_ref, wx_ref, bx_ref, ap_ref,
                     o_ref, hnew_ref):
    sp = _softplus(-ap_ref[...])
    for n in range(LRU_BLOCKS):
        cs = slice(n * LRU_BLOCK, (n + 1) * LRU_BLOCK)
        xc = cb_ref[:, cs] + conv_ref[0, :, cs] * cw_ref[0:1, cs]
        for tap in range(1, CONV_WIDTH - 1):
            xc = xc + conv_ref[tap, :, cs] * cw_ref[tap:tap + 1, cs]
        xc = xc + xr_ref[:, cs] * cw_ref[CONV_WIDTH - 1:CONV_WIDTH, cs]
        a, b = _lru_gates(xc, wa_ref[n], ba_ref[n], wx_ref[n], bx_ref[n], sp[:, cs])
        h = a * h0_ref[:, cs] + b
        hnew_ref[:, cs] = h
        o_ref[:, cs] = h * yg_ref[:, cs]


def _lru_step(proj, conv_t, h0, cw, cb, wa, ba, wx, bx, ap):
    m = proj.shape[0]
    w = LRU_WIDTH
    c2 = lambda i: (0, 0)
    c3 = lambda i: (0, 0, 0)
    return pl.pallas_call(
        _lru_step_kernel,
        grid=(1,),
        in_specs=[
            pl.BlockSpec((m, w), lambda i: (0, 1)),
            pl.BlockSpec((m, w), lambda i: (0, 0)),
            pl.BlockSpec((CONV_WIDTH - 1, m, w), c3),
            pl.BlockSpec((m, w), c2),
            pl.BlockSpec((CONV_WIDTH, w), c2),
            pl.BlockSpec((1, w), c2),
            pl.BlockSpec((LRU_BLOCKS, LRU_BLOCK, LRU_BLOCK), c3),
            pl.BlockSpec((LRU_BLOCKS, 1, LRU_BLOCK), c3),
            pl.BlockSpec((LRU_BLOCKS, LRU_BLOCK, LRU_BLOCK), c3),
            pl.BlockSpec((LRU_BLOCKS, 1, LRU_BLOCK), c3),
            pl.BlockSpec((1, w), c2),
        ],
        out_specs=[pl.BlockSpec((m, w), c2), pl.BlockSpec((m, w), c2)],
        out_shape=[jax.ShapeDtypeStruct((m, w), F32), jax.ShapeDtypeStruct((m, w), F32)],
        compiler_params=_params(1),
        name="lru_step",
    )(proj, proj, conv_t, h0, cw, cb, wa, ba, wx, bx, ap)


def _lambda(lam_ref, lam_init):
    s1 = jnp.sum(lam_ref[0:1, :] * lam_ref[1:2, :], axis=-1, keepdims=True)
    s2 = jnp.sum(lam_ref[2:3, :] * lam_ref[3:4, :], axis=-1, keepdims=True)
    return jnp.exp(s1) - jnp.exp(s2) + lam_init


def _subln(o, g, lam_init):
    return o * _rms_scale(o, SUBLN_EPS) * g * (1.0 - lam_init)


def _nt_dot(a, b):
    return lax.dot_general(a, b, (((1,), (1,)), ((), ())), preferred_element_type=F32)


def _attn_prompt_kernel(lam_ref, g_ref, q_ref, k_ref, v_ref, o_ref, qb_ref, kb_ref, vb_ref, m_ref, l_ref, acc_ref,
                        *, tile, lam_init):
    qi = pl.program_id(2)
    maps = range(2)
    hs = [slice(s * HEAD_DIM, (s + 1) * HEAD_DIM) for s in maps]
    groups = tile // LANES

    @pl.when(qi == 0)
    def _():
        kb_ref[...] = k_ref[...].astype(BF16)
        vb_ref[...] = v_ref[...].astype(BF16)

    qb_ref[...] = q_ref[...].astype(BF16)

    def scores(c, diagonal):
        kc = kb_ref[pl.ds(pl.multiple_of(c * tile, tile), tile), :]
        sc = [_nt_dot(qb_ref[:, hs[s]], kc[:, hs[s]]) for s in maps]
        if diagonal:
            keep = (lax.broadcasted_iota(jnp.int32, (tile, tile), 1)
                    <= lax.broadcasted_iota(jnp.int32, (tile, tile), 0))
            sc = [jnp.where(keep, x, -jnp.inf) for x in sc]
        return sc

    def lane_groups(x):
        return [x[:, j * LANES:(j + 1) * LANES] for j in range(groups)]

    def max_pass(c, diagonal):
        sc = scores(c, diagonal)
        m_old = [m_ref[s] for s in maps]
        for s in maps:
            m_ref[s] = functools.reduce(jnp.maximum, lane_groups(sc[s]), m_old[s])

    def sum_pass(c, diagonal):
        sc = scores(c, diagonal)
        vc = vb_ref[pl.ds(pl.multiple_of(c * tile, tile), tile), :]
        m_row = [m_ref[s] for s in maps]
        l_old = [l_ref[s] for s in maps]
        acc_old = [acc_ref[s] for s in maps]
        p = [[jnp.exp2(x - m_row[s]) for x in lane_groups(sc[s])] for s in maps]
        for s in maps:
            l_ref[s] = functools.reduce(jnp.add, p[s], l_old[s])
            pb = jnp.concatenate(p[s], axis=1).astype(BF16)
            acc_ref[s] = acc_old[s] + jnp.dot(pb, vc, preferred_element_type=F32)

    def loop(body):
        def step(c, carry):
            body(c, False)
            return carry
        lax.fori_loop(0, qi, step, 0)
        body(qi, True)

    m_ref[...] = jnp.full(m_ref.shape, -jnp.inf, F32)
    loop(max_pass)
    for s in maps:
        m_ref[s] = jnp.broadcast_to(jnp.max(m_ref[s], axis=-1, keepdims=True), (tile, LANES))
    l_ref[...] = jnp.zeros(l_ref.shape, F32)
    acc_ref[...] = jnp.zeros(acc_ref.shape, F32)
    loop(sum_pass)

    lam = _lambda(lam_ref, lam_init)
    inv = [1.0 / jnp.sum(l_ref[s], axis=-1, keepdims=True) for s in maps]
    o = acc_ref[0] * inv[0] - lam * (acc_ref[1] * inv[1])
    o_ref[...] = _subln(o, g_ref[...], lam_init).astype(o_ref.dtype)


def _attn_prompt(qkv, lam4, g, *, batch, seq, tile, lam_init):
    hw = 2 * HEAD_DIM
    nq = seq // tile
    part = D_MODEL // hw
    return pl.pallas_call(
        functools.partial(_attn_prompt_kernel, tile=tile, lam_init=lam_init),
        grid=(batch, N_HEADS, nq),
        in_specs=[
            pl.BlockSpec((4, HEAD_DIM), lambda b, h, i: (0, 0)),
            pl.BlockSpec((1, hw), lambda b, h, i: (0, 0)),
            pl.BlockSpec((tile, hw), lambda b, h, i: (b * nq + i, h)),
            pl.BlockSpec((seq, hw), lambda b, h, i: (b, part + h)),
            pl.BlockSpec((seq, hw), lambda b, h, i: (b, 2 * part + h)),
        ],
        out_specs=pl.BlockSpec((tile, hw), lambda b, h, i: (b * nq + i, h)),
        out_shape=jax.ShapeDtypeStruct((batch * seq, D_MODEL), BF16),
        scratch_shapes=[
            pltpu.VMEM((tile, hw), BF16),
            pltpu.VMEM((seq, hw), BF16),
            pltpu.VMEM((seq, hw), BF16),
            pltpu.VMEM((2, tile, LANES), F32),
            pltpu.VMEM((2, tile, LANES), F32),
            pltpu.VMEM((2, tile, hw), F32),
        ],
        compiler_params=_params(3),
        name="attn_prompt",
    )(lam4, g, qkv, qkv, qkv)


def _attn_decode_kernel(pt_ref, lam_ref, g_ref, q_ref, knew_ref, vnew_ref, *rest, pages_per_step, page, lam_init):
    del pt_ref
    npg = pages_per_step
    k_refs = rest[:npg]
    v_refs = rest[npg:2 * npg]
    o_ref, m_ref, l_ref, acc_ref = rest[2 * npg:]
    step = pl.program_id(1)
    n_maps = 2 * N_HEADS
    cols = page * N_HEADS

    @pl.when(step == 0)
    def _():
        m_ref[...] = jnp.full(m_ref.shape, -jnp.inf, F32)
        l_ref[...] = jnp.zeros(l_ref.shape, F32)
        acc_ref[...] = jnp.zeros(acc_ref.shape, F32)

    q = q_ref[0]
    qb = q.astype(BF16)
    row = lax.broadcasted_iota(jnp.int32, (n_maps, cols), 0)
    col = lax.broadcasted_iota(jnp.int32, (n_maps, cols), 1)
    first_map = row < N_HEADS
    own = (col & (N_HEADS - 1)) == (row & (N_HEADS - 1))

    scores = []
    for j in range(npg):
        k_even = k_refs[j][:, pl.ds(0, N_HEADS, stride=2), :].reshape(cols, HEAD_DIM).astype(BF16)
        k_odd = k_refs[j][:, pl.ds(1, N_HEADS, stride=2), :].reshape(cols, HEAD_DIM).astype(BF16)
        sc = jnp.where(first_map, _nt_dot(qb, k_even), _nt_dot(qb, k_odd))
        scores.append(jnp.where(own, sc, -jnp.inf))
    sc = jnp.concatenate(scores, axis=1) if npg > 1 else scores[0]
    m_old = m_ref[...]
    m_new = jnp.maximum(m_old, jnp.max(sc, axis=-1, keepdims=True))
    alpha = jnp.exp2(m_old - m_new)
    p = jnp.exp2(sc - m_new)
    l_ref[...] = alpha * l_ref[...] + jnp.sum(p, axis=-1, keepdims=True)
    pb = p.astype(BF16)
    pv = None
    for j in range(npg):
        vj = v_refs[j][...].reshape(cols, 2 * HEAD_DIM).astype(BF16)
        part = jnp.dot(pb[:, j * cols:(j + 1) * cols], vj, preferred_element_type=F32)
        pv = part if pv is None else pv + part
    acc_ref[...] = alpha * acc_ref[...] + pv
    m_ref[...] = m_new

    @pl.when(step == pl.num_programs(1) - 1)
    def _():
        s_new = jnp.sum(q * knew_ref[0], axis=-1, keepdims=True)
        m_old2 = m_ref[...]
        m_fin = jnp.maximum(m_old2, s_new)
        alpha2 = jnp.exp2(m_old2 - m_fin)
        p_new = jnp.exp2(s_new - m_fin)
        l_fin = alpha2 * l_ref[...] + p_new
        v_new = vnew_ref[0]
        acc = (alpha2 * acc_ref[...] + p_new * jnp.concatenate([v_new, v_new], axis=0)) / l_fin
        lam = _lambda(lam_ref, lam_init)
        o = acc[0:N_HEADS, :] - lam * acc[N_HEADS:n_maps, :]
        o_ref[0] = _subln(o, g_ref[...], lam_init)


def _attn_decode(q, k_new, v_new, cache_k, cache_v, layer, page_table, lam4, g, *, pages_per_step, lam_init):
    rows = q.shape[0]
    n_pages = page_table.shape[1]
    page = cache_k.shape[2]
    npg = pages_per_step
    hw = 2 * HEAD_DIM
    n_maps = 2 * N_HEADS

    def page_spec(j, heads, width):
        return pl.BlockSpec((None, None, page, heads, width),
                            lambda b, s, pt: (layer, pt[b, s * npg + j], 0, 0, 0))

    qk_spec = pl.BlockSpec((1, n_maps, HEAD_DIM), lambda b, s, pt: (b, 0, 0))
    v_spec = pl.BlockSpec((1, N_HEADS, hw), lambda b, s, pt: (b, 0, 0))
    grid_spec = pltpu.PrefetchScalarGridSpec(
        num_scalar_prefetch=1,
        grid=(rows, n_pages // npg),
        in_specs=[
            pl.BlockSpec((4, HEAD_DIM), lambda b, s, pt: (0, 0)),
            pl.BlockSpec((1, hw), lambda b, s, pt: (0, 0)),
            qk_spec, qk_spec, v_spec,
        ] + [page_spec(j, n_maps, HEAD_DIM) for j in range(npg)] + [page_spec(j, N_HEADS, hw) for j in range(npg)],
        out_specs=v_spec,
        scratch_shapes=[
            pltpu.VMEM((n_maps, 1), F32),
            pltpu.VMEM((n_maps, 1), F32),
            pltpu.VMEM((n_maps, hw), F32),
        ],
    )
    return pl.pallas_call(
        functools.partial(_attn_decode_kernel, pages_per_step=npg, page=page, lam_init=lam_init),
        grid_spec=grid_spec,
        out_shape=jax.ShapeDtypeStruct((rows, N_HEADS, hw), F32),
        compiler_params=_params(2),
        name="attn_decode",
    )(page_table, lam4, g, q, k_new, v_new, *([cache_k] * npg), *([cache_v] * npg))


def _rope_tables(pos):
    half = HEAD_DIM // 2
    inv_freq = jnp.power(ROPE_THETA, -jnp.arange(half, dtype=F32) * (2.0 / HEAD_DIM))
    ang = pos.astype(F32)[:, None] * inv_freq[None, :]
    cos = jnp.cos(ang)
    sin = jnp.sin(ang)
    return jnp.concatenate([cos, cos], axis=-1), jnp.concatenate([-sin, sin], axis=-1)


def kernel(x_prompt, x_sample, state_conv, state_h, cache_k, cache_v, page_table, norm_mix_pre, norm_mix_post, norm_ffn_pre, norm_ffn_post, lru_w_in, lru_b_in, lru_conv_w, lru_conv_b, lru_w_gate_a, lru_b_gate_a, lru_w_gate_x, lru_b_gate_x, lru_a_param, lru_w_out, attn_w_qkv, attn_lambda_q1, attn_lambda_k1, attn_lambda_q2, attn_lambda_k2, attn_subln, attn_w_o, ffn_w_gate, ffn_w_up, ffn_w_down):
    bp, seq, d = x_prompt.shape
    bd, t_dec, _ = x_sample.shape
    assert d == D_MODEL and t_dec == 1
    depth = norm_mix_pre.shape[0]
    n_pages = page_table.shape[1]
    page = cache_k.shape[2]
    past = n_pages * page
    mp = bp * seq

    xp = x_prompt.reshape(mp, d)
    xs = x_sample.reshape(bd, d)
    row = lambda v: v.reshape(1, -1)

    conv_p, h_p, k_p, v_p = [], [], [], []
    conv_s, h_s, k_s, v_s = [], [], [], []
    for i in range(depth):
        j = i // N_MIXERS
        g_pre = row(norm_mix_pre[i])
        g_post = row(norm_mix_post[i])
        if i % N_MIXERS == 0:
            w_in = lru_w_in[j].astype(BF16)
            b_in = row(lru_b_in[j])
            w_out = lru_w_out[j].astype(BF16)
            wa = lru_w_gate_a[j].astype(BF16)
            wx = lru_w_gate_x[j].astype(BF16)
            ba = lru_b_gate_a[j].reshape(LRU_BLOCKS, 1, LRU_BLOCK)
            bx = lru_b_gate_x[j].reshape(LRU_BLOCKS, 1, LRU_BLOCK)
            gate_prm = (lru_conv_w[j], row(lru_conv_b[j]), wa, ba, wx, bx, row(lru_a_param[j]))
            proj_p = _lru_in_proj(xp, g_pre, w_in, b_in, tm=1024, tn=1024)
            hy_p, hl_p = _lru_seq(proj_p, jnp.zeros((bp, SUBLANES, LRU_WIDTH), F32), jnp.zeros((bp, 1, LRU_WIDTH), F32),
                                  *gate_prm, batch=bp, seq=seq, tt=256)
            conv_p.append(proj_p.reshape(bp, seq, 2 * LRU_WIDTH)[:, seq - (CONV_WIDTH - 1):, LRU_WIDTH:])
            h_p.append(hl_p.reshape(bp, LRU_WIDTH))
            xp = _out_proj_residual(hy_p, w_out, g_post, xp, tm=512)
            proj_s = _lru_in_proj(xs, g_pre, w_in, b_in, tm=bd, tn=1024)
            hy_s, hl_s = _lru_step(proj_s, state_conv[j].transpose(1, 0, 2), state_h[j], *gate_prm)
            conv_s.append(jnp.concatenate([state_conv[j][:, 1:], proj_s[:, None, LRU_WIDTH:]], axis=1))
            h_s.append(hl_s)
            xs = _out_proj_residual(hy_s, w_out, g_post, xs, tm=bd)
        else:
            lam_init = 0.8 - 0.6 * math.exp(-0.3 * i)
            w_qkv = attn_w_qkv[j].astype(BF16)
            w_o = attn_w_o[j].astype(BF16)
            lam4 = jnp.stack([attn_lambda_q1[j], attn_lambda_k1[j], attn_lambda_q2[j], attn_lambda_k2[j]])
            g_sub = row(attn_subln[j])
            cos_p, sin_p = _rope_tables(jnp.arange(seq))
            qkv_p, k4_p, v4_p = _qkv_proj(xp, g_pre, w_qkv, cos_p, sin_p, tm=1024, tn=1024)
            o = _attn_prompt(qkv_p, lam4, g_sub, batch=bp, seq=seq, tile=512, lam_init=lam_init)
            k_p.append(k4_p.reshape(bp, seq, 2 * N_HEADS, HEAD_DIM))
            v_p.append(v4_p.reshape(bp, seq, N_HEADS, 2 * HEAD_DIM))
            xp = _out_proj_residual(o, w_o, g_post, xp, tm=512)
            cos_s, sin_s = _rope_tables(jnp.full((bd,), past))
            qkv_s, ks_new, vs_new = _qkv_proj(xs, g_pre, w_qkv, cos_s, sin_s, tm=bd, tn=1024)
            map_major = lambda a: a.reshape(bd, N_HEADS, 2, HEAD_DIM).transpose(0, 2, 1, 3).reshape(bd, 2 * N_HEADS, HEAD_DIM)
            os_ = _attn_decode(map_major(qkv_s[:, :d]), map_major(ks_new), vs_new,
                               cache_k, cache_v, j, page_table, lam4, g_sub, pages_per_step=8, lam_init=lam_init)
            k_s.append(ks_new.reshape(bd, 1, 2 * N_HEADS, HEAD_DIM))
            v_s.append(vs_new.reshape(bd, 1, N_HEADS, 2 * HEAD_DIM))
            xs = _out_proj_residual(os_.reshape(bd, d), w_o, g_post, xs, tm=bd)
        w_gate, w_up, w_down = ffn_w_gate[i], ffn_w_up[i], ffn_w_down[i]
        gf_pre = row(norm_ffn_pre[i])
        gf_post = row(norm_ffn_post[i])
        xp = _ffn(xp, gf_pre, w_gate, w_up, w_down, gf_post, tm=1024, tf=256)
        xs = _ffn(xs, gf_pre, w_gate, w_up, w_down, gf_post, tm=bd, tf=512)
    return (xp.reshape(bp, seq, d), xs.reshape(bd, 1, d), jnp.stack(conv_p), jnp.stack(h_p), jnp.stack(k_p), jnp.stack(v_p),
            jnp.stack(conv_s), jnp.stack(h_s), jnp.stack(k_s), jnp.stack(v_s))
```

```python
import functools
import math

import jax
import jax.numpy as jnp
from jax import lax
from jax.experimental import pallas as pl
from jax.experimental.pallas import tpu as pltpu

D_MODEL = 2048
LRU_WIDTH = D_MODEL
LRU_BLOCK = 256
LRU_BLOCKS = LRU_WIDTH // LRU_BLOCK
CONV_WIDTH = 4
LRU_C = 8.0
HEAD_DIM = 128
N_HEADS = D_MODEL // (2 * HEAD_DIM)
ROPE_THETA = 10000.0
NORM_EPS = 1e-6
SUBLN_EPS = 1e-5
N_MIXERS = 2
Q_SCALE = HEAD_DIM ** -0.5 * math.log2(math.e)

SUBLANES = 8
LANES = 128
VMEM_CAPACITY_BYTES = 64 * 1024 * 1024
VMEM_LIMIT_BYTES = VMEM_CAPACITY_BYTES - 4 * 1024 * 1024

BF16 = jnp.bfloat16
F32 = jnp.float32


def _params(n_grid_dims):
    return pltpu.CompilerParams(
        dimension_semantics=("arbitrary",) * n_grid_dims,
        vmem_limit_bytes=VMEM_LIMIT_BYTES,
    )


def _rms_scale(x, eps):
    return lax.rsqrt(jnp.mean(x * x, axis=-1, keepdims=True) + eps)


def _sigmoid(x):
    return 1.0 / (1.0 + jnp.exp(-x))


def _gelu_tanh(x):
    c = math.sqrt(2.0 / math.pi)
    return 0.5 * x * (1.0 + jnp.tanh(c * (x + 0.044715 * (x * x * x))))


def _softplus(z):
    return jnp.maximum(z, 0.0) + jnp.log1p(jnp.exp(-jnp.abs(z)))


def _norm_once(x_ref, g_ref, h_ref):
    @pl.when(pl.program_id(1) == 0)
    def _():
        x = x_ref[...]
        h_ref[...] = (x * _rms_scale(x, NORM_EPS) * g_ref[...]).astype(BF16)


def _lru_in_kernel(x_ref, g_ref, w_ref, b_ref, o_ref, h_ref, *, n_gate_tiles):
    _norm_once(x_ref, g_ref, h_ref)
    j = pl.program_id(1)
    proj = lambda: jnp.dot(h_ref[...], w_ref[...], preferred_element_type=F32) + b_ref[...]

    @pl.when(j < n_gate_tiles)
    def _():
        o_ref[...] = _gelu_tanh(proj())

    @pl.when(j >= n_gate_tiles)
    def _():
        o_ref[...] = proj()


def _lru_in_proj(x, g, w, b, *, tm, tn):
    m, d = x.shape
    n = w.shape[1]
    return pl.pallas_call(
        functools.partial(_lru_in_kernel, n_gate_tiles=LRU_WIDTH // tn),
        grid=(m // tm, n // tn),
        in_specs=[
            pl.BlockSpec((tm, d), lambda i, j: (i, 0), pipeline_mode=pl.Buffered(1)),
            pl.BlockSpec((1, d), lambda i, j: (0, 0)),
            pl.BlockSpec((d, tn), lambda i, j: (0, j)),
            pl.BlockSpec((1, tn), lambda i, j: (0, j)),
        ],
        out_specs=pl.BlockSpec((tm, tn), lambda i, j: (i, j)),
        out_shape=jax.ShapeDtypeStruct((m, n), F32),
        scratch_shapes=[pltpu.VMEM((tm, d), BF16)],
        compiler_params=_params(2),
        name="lru_in_proj",
    )(x, g, w, b)


def _rope(x, cos2, sin2):
    return x * cos2 + pltpu.roll(x, HEAD_DIM // 2, axis=1) * sin2


def _qkv_kernel(x_ref, g_ref, w_ref, cos_ref, sin_ref, o_ref, k4_ref, v4_ref, h_ref, *, tiles_per_part, heads_per_tile):
    _norm_once(x_ref, g_ref, h_ref)
    j = pl.program_id(1)
    tm = o_ref.shape[0]
    proj = lambda: jnp.dot(h_ref[...], w_ref[...], preferred_element_type=F32)

    def roped(acc, scale):
        cos2 = cos_ref[...]
        sin2 = sin_ref[...]
        parts = []
        for hd in range(heads_per_tile):
            r = _rope(acc[:, hd * HEAD_DIM:(hd + 1) * HEAD_DIM], cos2, sin2)
            parts.append(r if scale is None else r * scale)
        return jnp.concatenate(parts, axis=1)

    @pl.when(j < tiles_per_part)
    def _():
        o_ref[...] = roped(proj(), Q_SCALE)

    @pl.when((j >= tiles_per_part) & (j < 2 * tiles_per_part))
    def _():
        k = roped(proj(), None)
        o_ref[...] = k
        k4_ref[...] = k.reshape(tm, heads_per_tile, HEAD_DIM)

    v_heads = heads_per_tile // 2
    for t in range(tiles_per_part):
        @pl.when(j == 2 * tiles_per_part + t)
        def _(t=t):
            v = proj()
            o_ref[...] = v
            v4_ref[:, t * v_heads:(t + 1) * v_heads, :] = v.reshape(tm, v_heads, 2 * HEAD_DIM)


def _qkv_proj(x, g, w, cos2, sin2, *, tm, tn):
    m, d = x.shape
    n = w.shape[1]
    tpp = d // tn
    hpt = tn // HEAD_DIM
    n_pos_tiles = cos2.shape[0] // tm
    return pl.pallas_call(
        functools.partial(_qkv_kernel, tiles_per_part=tpp, heads_per_tile=hpt),
        grid=(m // tm, n // tn),
        in_specs=[
            pl.BlockSpec((tm, d), lambda i, j: (i, 0), pipeline_mode=pl.Buffered(1)),
            pl.BlockSpec((1, d), lambda i, j: (0, 0)),
            pl.BlockSpec((d, tn), lambda i, j: (0, j)),
            pl.BlockSpec((tm, HEAD_DIM), lambda i, j: (i % n_pos_tiles, 0)),
            pl.BlockSpec((tm, HEAD_DIM), lambda i, j: (i % n_pos_tiles, 0)),
        ],
        out_specs=[
            pl.BlockSpec((tm, tn), lambda i, j: (i, j)),
            pl.BlockSpec((tm, hpt, HEAD_DIM), lambda i, j: (i, jnp.clip(j - tpp, 0, tpp - 1), 0)),
            pl.BlockSpec((tm, N_HEADS, 2 * HEAD_DIM), lambda i, j: (i, 0, 0)),
        ],
        out_shape=[
            jax.ShapeDtypeStruct((m, n), F32),
            jax.ShapeDtypeStruct((m, 2 * N_HEADS, HEAD_DIM), F32),
            jax.ShapeDtypeStruct((m, N_HEADS, 2 * HEAD_DIM), F32),
        ],
        scratch_shapes=[pltpu.VMEM((tm, d), BF16)],
        compiler_params=_params(2),
        name="qkv_proj",
    )(x, g, w, cos2, sin2)


def _out_proj_kernel(a_ref, w_ref, g_ref, x_ref, o_ref):
    y = jnp.dot(a_ref[...].astype(BF16), w_ref[...], preferred_element_type=F32)
    o_ref[...] = x_ref[...] + y * _rms_scale(y, NORM_EPS) * g_ref[...]


def _out_proj_residual(a, w, g, x, *, tm):
    m, k = a.shape
    d = w.shape[1]
    return pl.pallas_call(
        _out_proj_kernel,
        grid=(m // tm,),
        in_specs=[
            pl.BlockSpec((tm, k), lambda i: (i, 0)),
            pl.BlockSpec((k, d), lambda i: (0, 0)),
            pl.BlockSpec((1, d), lambda i: (0, 0)),
            pl.BlockSpec((tm, d), lambda i: (i, 0)),
        ],
        out_specs=pl.BlockSpec((tm, d), lambda i: (i, 0)),
        out_shape=jax.ShapeDtypeStruct((m, d), F32),
        compiler_params=_params(1),
        name="out_proj_residual",
    )(a, w, g, x)


def _ffn_kernel(x_ref, gpre_ref, wg_ref, wu_ref, wd_ref, gpost_ref, o_ref, h_ref):
    f = pl.program_id(1)

    @pl.when(f == 0)
    def _():
        x = x_ref[...]
        h_ref[...] = (x * _rms_scale(x, NORM_EPS) * gpre_ref[...]).astype(BF16)
        o_ref[...] = jnp.zeros(o_ref.shape, F32)

    h = h_ref[...]
    gate = jnp.dot(h, wg_ref[...].astype(BF16), preferred_element_type=F32)
    up = jnp.dot(h, wu_ref[...].astype(BF16), preferred_element_type=F32)
    act = (gate * _sigmoid(gate) * up).astype(BF16)
    o_ref[...] += jnp.dot(act, wd_ref[...].astype(BF16), preferred_element_type=F32)

    @pl.when(f == pl.num_programs(1) - 1)
    def _():
        y = o_ref[...]
        o_ref[...] = x_ref[...] + y * _rms_scale(y, NORM_EPS) * gpost_ref[...]


def _ffn(x, gpre, wg, wu, wd, gpost, layer, *, tm, tf):
    m, d = x.shape
    dff = wg.shape[2]
    rows_once = dict(pipeline_mode=pl.Buffered(1))
    return pl.pallas_call(
        _ffn_kernel,
        grid=(m // tm, dff // tf),
        in_specs=[
            pl.BlockSpec((tm, d), lambda i, f: (i, 0), **rows_once),
            pl.BlockSpec((1, d), lambda i, f: (0, 0)),
            pl.BlockSpec((None, d, tf), lambda i, f: (layer, 0, f)),
            pl.BlockSpec((None, d, tf), lambda i, f: (layer, 0, f)),
            pl.BlockSpec((None, tf, d), lambda i, f: (layer, f, 0)),
            pl.BlockSpec((1, d), lambda i, f: (0, 0)),
        ],
        out_specs=pl.BlockSpec((tm, d), lambda i, f: (i, 0), **rows_once),
        out_shape=jax.ShapeDtypeStruct((m, d), F32),
        scratch_shapes=[pltpu.VMEM((tm, d), BF16)],
        compiler_params=_params(2),
        name="ffn",
    )(x, gpre, wg, wu, wd, gpost)


def _lru_gates(xc, wa, ba, wx, bx, sp):
    xcb = xc.astype(BF16)
    gate_a = _sigmoid(jnp.dot(xcb, wa, preferred_element_type=F32) + ba)
    gate_x = _sigmoid(jnp.dot(xcb, wx, preferred_element_type=F32) + bx)
    log_a = (-LRU_C * gate_a) * sp
    a = jnp.exp(log_a)
    b = jnp.sqrt(-jnp.tanh(log_a) * (a * a + 1.0)) * (gate_x * xc)
    return a, b


def _lru_seq_kernel(xr_ref, yg_ref, conv0_ref, h0_ref, cw_ref, cb_ref, wa_ref, ba_ref, wx_ref, bx_ref, ap_ref,
                    o_ref, hlast_ref, xe_ref, h_ref, a_ref, b_ref, *, tt):
    ti = pl.program_id(1)
    pad = SUBLANES

    @pl.when(ti == 0)
    def _():
        xe_ref[0:pad, :] = conv0_ref[0]
        h_ref[...] = h0_ref[0]

    xe_ref[pad:pad + tt, :] = xr_ref[...]
    sp = _softplus(-ap_ref[...])
    rowmod = lax.broadcasted_iota(jnp.int32, (tt // SUBLANES, SUBLANES, LRU_BLOCK), 1)

    for n in range(LRU_BLOCKS):
        cs = slice(n * LRU_BLOCK, (n + 1) * LRU_BLOCK)
        xc = cb_ref[:, cs] + xe_ref[pad - 3:pad - 3 + tt, cs] * cw_ref[0:1, cs]
        for tap in range(1, CONV_WIDTH):
            xc = xc + xe_ref[pad - 3 + tap:pad - 3 + tap + tt, cs] * cw_ref[tap:tap + 1, cs]
        a, b = _lru_gates(xc, wa_ref[n], ba_ref[n], wx_ref[n], bx_ref[n], sp[:, cs])
        a = a.reshape(tt // SUBLANES, SUBLANES, LRU_BLOCK)
        b = b.reshape(tt // SUBLANES, SUBLANES, LRU_BLOCK)
        d = 1
        while d < SUBLANES:
            valid = rowmod >= d
            a_sh = jnp.where(valid, pltpu.roll(a, d, axis=1), 1.0)
            b_sh = jnp.where(valid, pltpu.roll(b, d, axis=1), 0.0)
            b = a * b_sh + b
            a = a * a_sh
            d *= 2
        a_ref[:, cs] = a.reshape(tt, LRU_BLOCK)
        b_ref[:, cs] = b.reshape(tt, LRU_BLOCK)

    def group(gi, carry):
        r = pl.multiple_of(gi * SUBLANES, SUBLANES)
        h = a_ref[pl.ds(r, SUBLANES), :] * carry + b_ref[pl.ds(r, SUBLANES), :]
        b_ref[pl.ds(r, SUBLANES), :] = h
        return h[SUBLANES - 1:SUBLANES, :]

    carry = lax.fori_loop(0, tt // SUBLANES, group, h_ref[...], unroll=4)
    h_ref[...] = carry
    hlast_ref[0] = carry
    o_ref[...] = (b_ref[...] * yg_ref[...]).astype(o_ref.dtype)
    xe_ref[0:pad, :] = xe_ref[tt:tt + pad, :]


def _lru_seq(proj, conv0, h0, cw, cb, wa, ba, wx, bx, ap, *, batch, seq, tt):
    w = LRU_WIDTH
    nt = seq // tt
    const2 = lambda b, t: (0, 0)
    const3 = lambda b, t: (0, 0, 0)
    return pl.pallas_call(
        functools.partial(_lru_seq_kernel, tt=tt),
        grid=(batch, nt),
        in_specs=[
            pl.BlockSpec((tt, w), lambda b, t: (b * nt + t, 1)),
            pl.BlockSpec((tt, w), lambda b, t: (b * nt + t, 0)),
            pl.BlockSpec((1, SUBLANES, w), lambda b, t: (b, 0, 0)),
            pl.BlockSpec((1, 1, w), lambda b, t: (b, 0, 0)),
            pl.BlockSpec((CONV_WIDTH, w), const2),
            pl.BlockSpec((1, w), const2),
            pl.BlockSpec((LRU_BLOCKS, LRU_BLOCK, LRU_BLOCK), const3),
            pl.BlockSpec((LRU_BLOCKS, 1, LRU_BLOCK), const3),
            pl.BlockSpec((LRU_BLOCKS, LRU_BLOCK, LRU_BLOCK), const3),
            pl.BlockSpec((LRU_BLOCKS, 1, LRU_BLOCK), const3),
            pl.BlockSpec((1, w), const2),
        ],
        out_specs=[
            pl.BlockSpec((tt, w), lambda b, t: (b * nt + t, 0)),
            pl.BlockSpec((1, 1, w), lambda b, t: (b, 0, 0)),
        ],
        out_shape=[
            jax.ShapeDtypeStruct((batch * seq, w), BF16),
            jax.ShapeDtypeStruct((batch, 1, w), F32),
        ],
        scratch_shapes=[
            pltpu.VMEM((tt + SUBLANES, w), F32),
            pltpu.VMEM((1, w), F32),
            pltpu.VMEM((tt, w), F32),
            pltpu.VMEM((tt, w), F32),
        ],
        compiler_params=_params(2),
        name="lru_seq",
    )(proj, proj, conv0, h0, cw, cb, wa, ba, wx, bx, ap)


def _lru_step_kernel(xr_ref, yg_ref, conv_ref, h0_ref, cw_ref, cb_ref, wa_ref, ba_ref, wx_ref, bx_ref, ap_ref,
                     o_ref, hnew_ref):
    sp = _softplus(-ap_ref[...])
    for n in range(LRU_BLOCKS):
        cs = slice(n * LRU_BLOCK, (n + 1) * LRU_BLOCK)
        xc = cb_ref[:, cs] + conv_ref[0, :, cs] * cw_ref[0:1, cs]
        for tap in range(1, CONV_WIDTH - 1):
            xc = xc + conv_ref[tap, :, cs] * cw_ref[tap:tap + 1, cs]
        xc = xc + xr_ref[:, cs] * cw_ref[CONV_WIDTH - 1:CONV_WIDTH, cs]
        a, b = _lru_gates(xc, wa_ref[n], ba_ref[n], wx_ref[n], bx_ref[n], sp[:, cs])
        h = a * h0_ref[:, cs] + b
        hnew_ref[:, cs] = h
        o_ref[:, cs] = h * yg_ref[:, cs]


def _lru_step(proj, conv_t, h0, cw, cb, wa, ba, wx, bx, ap):
    m = proj.shape[0]
    w = LRU_WIDTH
    c2 = lambda i: (0, 0)
    c3 = lambda i: (0, 0, 0)
    return pl.pallas_call(
        _lru_step_kernel,
        grid=(1,),
        in_specs=[
            pl.BlockSpec((m, w), lambda i: (0, 1)),
            pl.BlockSpec((m, w), lambda i: (0, 0)),
            pl.BlockSpec((CONV_WIDTH - 1, m, w), c3),
            pl.BlockSpec((m, w), c2),
            pl.BlockSpec((CONV_WIDTH, w), c2),
            pl.BlockSpec((1, w), c2),
            pl.BlockSpec((LRU_BLOCKS, LRU_BLOCK, LRU_BLOCK), c3),
            pl.BlockSpec((LRU_BLOCKS, 1, LRU_BLOCK), c3),
            pl.BlockSpec((LRU_BLOCKS, LRU_BLOCK, LRU_BLOCK), c3),
            pl.BlockSpec((LRU_BLOCKS, 1, LRU_BLOCK), c3),
            pl.BlockSpec((1, w), c2),
        ],
        out_specs=[pl.BlockSpec((m, w), c2), pl.BlockSpec((m, w), c2)],
        out_shape=[jax.ShapeDtypeStruct((m, w), F32), jax.ShapeDtypeStruct((m, w), F32)],
        compiler_params=_params(1),
        name="lru_step",
    )(proj, proj, conv_t, h0, cw, cb, wa, ba, wx, bx, ap)


def _lambda(lam_ref, lam_init):
    s1 = jnp.sum(lam_ref[0:1, :] * lam_ref[1:2, :], axis=-1, keepdims=True)
    s2 = jnp.sum(lam_ref[2:3, :] * lam_ref[3:4, :], axis=-1, keepdims=True)
    return jnp.exp(s1) - jnp.exp(s2) + lam_init


def _subln(o, g, lam_init):
    return o * _rms_scale(o, SUBLN_EPS) * g * (1.0 - lam_init)


def _nt_dot(a, b):
    return lax.dot_general(a, b, (((1,), (1,)), ((), ())), preferred_element_type=F32)


def _attn_prompt_kernel(lam_ref, g_ref, q_ref, k_ref, v_ref, o_ref, qb_ref, kb_ref, vb_ref, m_ref, l_ref, acc_ref,
                        *, tile, lam_init):
    qi = pl.program_id(2)
    maps = range(2)
    hs = [slice(s * HEAD_DIM, (s + 1) * HEAD_DIM) for s in maps]
    groups = tile // LANES

    @pl.when(qi == 0)
    def _():
        kb_ref[...] = k_ref[...].astype(BF16)
        vb_ref[...] = v_ref[...].astype(BF16)

    qb_ref[...] = q_ref[...].astype(BF16)

    def scores(c, diagonal):
        kc = kb_ref[pl.ds(pl.multiple_of(c * tile, tile), tile), :]
        sc = [_nt_dot(qb_ref[:, hs[s]], kc[:, hs[s]]) for s in maps]
        if diagonal:
            keep = (lax.broadcasted_iota(jnp.int32, (tile, tile), 1)
                    <= lax.broadcasted_iota(jnp.int32, (tile, tile), 0))
            sc = [jnp.where(keep, x, -jnp.inf) for x in sc]
        return sc

    def lane_groups(x):
        return [x[:, j * LANES:(j + 1) * LANES] for j in range(groups)]

    def max_pass(c, diagonal):
        sc = scores(c, diagonal)
        m_old = [m_ref[s] for s in maps]
        for s in maps:
            m_ref[s] = functools.reduce(jnp.maximum, lane_groups(sc[s]), m_old[s])

    def sum_pass(c, diagonal):
        sc = scores(c, diagonal)
        vc = vb_ref[pl.ds(pl.multiple_of(c * tile, tile), tile), :]
        m_row = [m_ref[s] for s in maps]
        l_old = [l_ref[s] for s in maps]
        acc_old = [acc_ref[s] for s in maps]
        p = [[jnp.exp2(x - m_row[s]) for x in lane_groups(sc[s])] for s in maps]
        for s in maps:
            l_ref[s] = functools.reduce(jnp.add, p[s], l_old[s])
            pb = jnp.concatenate(p[s], axis=1).astype(BF16)
            acc_ref[s] = acc_old[s] + jnp.dot(pb, vc, preferred_element_type=F32)

    def loop(body):
        def step(c, carry):
            body(c, False)
            return carry
        lax.fori_loop(0, qi, step, 0)
        body(qi, True)

    m_ref[...] = jnp.full(m_ref.shape, -jnp.inf, F32)
    loop(max_pass)
    for s in maps:
        m_ref[s] = jnp.broadcast_to(jnp.max(m_ref[s], axis=-1, keepdims=True), (tile, LANES))
    l_ref[...] = jnp.zeros(l_ref.shape, F32)
    acc_ref[...] = jnp.zeros(acc_ref.shape, F32)
    loop(sum_pass)

    lam = _lambda(lam_ref, lam_init)
    inv = [1.0 / jnp.sum(l_ref[s], axis=-1, keepdims=True) for s in maps]
    o = acc_ref[0] * inv[0] - lam * (acc_ref[1] * inv[1])
    o_ref[...] = _subln(o, g_ref[...], lam_init).astype(o_ref.dtype)


def _attn_prompt(qkv, lam4, g, *, batch, seq, tile, lam_init):
    hw = 2 * HEAD_DIM
    nq = seq // tile
    part = D_MODEL // hw
    return pl.pallas_call(
        functools.partial(_attn_prompt_kernel, tile=tile, lam_init=lam_init),
        grid=(batch, N_HEADS, nq),
        in_specs=[
            pl.BlockSpec((4, HEAD_DIM), lambda b, h, i: (0, 0)),
            pl.BlockSpec((1, hw), lambda b, h, i: (0, 0)),
            pl.BlockSpec((tile, hw), lambda b, h, i: (b * nq + i, h)),
            pl.BlockSpec((seq, hw), lambda b, h, i: (b, part + h)),
            pl.BlockSpec((seq, hw), lambda b, h, i: (b, 2 * part + h)),
        ],
        out_specs=pl.BlockSpec((tile, hw), lambda b, h, i: (b * nq + i, h)),
        out_shape=jax.ShapeDtypeStruct((batch * seq, D_MODEL), BF16),
        scratch_shapes=[
            pltpu.VMEM((tile, hw), BF16),
            pltpu.VMEM((seq, hw), BF16),
            pltpu.VMEM((seq, hw), BF16),
            pltpu.VMEM((2, tile, LANES), F32),
            pltpu.VMEM((2, tile, LANES), F32),
            pltpu.VMEM((2, tile, hw), F32),
        ],
        compiler_params=_params(3),
        name="attn_prompt",
    )(lam4, g, qkv, qkv, qkv)


def _attn_decode_kernel(pt_ref, lam_ref, g_ref, q_ref, knew_ref, vnew_ref, *rest, pages_per_step, page, lam_init):
    del pt_ref
    npg = pages_per_step
    k_refs = rest[:npg]
    v_refs = rest[npg:2 * npg]
    o_ref, m_ref, l_ref, acc_ref = rest[2 * npg:]
    step = pl.program_id(1)
    n_maps = 2 * N_HEADS
    cols = page * N_HEADS

    @pl.when(step == 0)
    def _():
        m_ref[...] = jnp.full(m_ref.shape, -jnp.inf, F32)
        l_ref[...] = jnp.zeros(l_ref.shape, F32)
        acc_ref[...] = jnp.zeros(acc_ref.shape, F32)

    q = q_ref[0]
    qb = q.astype(BF16)
    row = lax.broadcasted_iota(jnp.int32, (n_maps, cols), 0)
    col = lax.broadcasted_iota(jnp.int32, (n_maps, cols), 1)
    first_map = row < N_HEADS
    own = (col & (N_HEADS - 1)) == (row & (N_HEADS - 1))

    scores = []
    for j in range(npg):
        k_even = k_refs[j][:, pl.ds(0, N_HEADS, stride=2), :].reshape(cols, HEAD_DIM).astype(BF16)
        k_odd = k_refs[j][:, pl.ds(1, N_HEADS, stride=2), :].reshape(cols, HEAD_DIM).astype(BF16)
        sc = jnp.where(first_map, _nt_dot(qb, k_even), _nt_dot(qb, k_odd))
        scores.append(jnp.where(own, sc, -jnp.inf))
    sc = jnp.concatenate(scores, axis=1) if npg > 1 else scores[0]
    m_old = m_ref[...]
    m_new = jnp.maximum(m_old, jnp.max(sc, axis=-1, keepdims=True))
    alpha = jnp.exp2(m_old - m_new)
    p = jnp.exp2(sc - m_new)
    l_ref[...] = alpha * l_ref[...] + jnp.sum(p, axis=-1, keepdims=True)
    pb = p.astype(BF16)
    pv = None
    for j in range(npg):
        vj = v_refs[j][...].reshape(cols, 2 * HEAD_DIM).astype(BF16)
        part = jnp.dot(pb[:, j * cols:(j + 1) * cols], vj, preferred_element_type=F32)
        pv = part if pv is None else pv + part
    acc_ref[...] = alpha * acc_ref[...] + pv
    m_ref[...] = m_new

    @pl.when(step == pl.num_programs(1) - 1)
    def _():
        s_new = jnp.sum(q * knew_ref[0], axis=-1, keepdims=True)
        m_old2 = m_ref[...]
        m_fin = jnp.maximum(m_old2, s_new)
        alpha2 = jnp.exp2(m_old2 - m_fin)
        p_new = jnp.exp2(s_new - m_fin)
        l_fin = alpha2 * l_ref[...] + p_new
        v_new = vnew_ref[0]
        acc = (alpha2 * acc_ref[...] + p_new * jnp.concatenate([v_new, v_new], axis=0)) / l_fin
        lam = _lambda(lam_ref, lam_init)
        o = acc[0:N_HEADS, :] - lam * acc[N_HEADS:n_maps, :]
        o_ref[0] = _subln(o, g_ref[...], lam_init)


def _attn_decode(q, k_new, v_new, cache_k, cache_v, layer, page_table, lam4, g, *, pages_per_step, lam_init):
    rows = q.shape[0]
    n_pages = page_table.shape[1]
    page = cache_k.shape[2]
    npg = pages_per_step
    hw = 2 * HEAD_DIM
    n_maps = 2 * N_HEADS

    def page_spec(j, heads, width):
        return pl.BlockSpec((None, None, page, heads, width),
                            lambda b, s, pt: (layer, pt[b, s * npg + j], 0, 0, 0))

    qk_spec = pl.BlockSpec((1, n_maps, HEAD_DIM), lambda b, s, pt: (b, 0, 0))
    v_spec = pl.BlockSpec((1, N_HEADS, hw), lambda b, s, pt: (b, 0, 0))
    grid_spec = pltpu.PrefetchScalarGridSpec(
        num_scalar_prefetch=1,
        grid=(rows, n_pages // npg),
        in_specs=[
            pl.BlockSpec((4, HEAD_DIM), lambda b, s, pt: (0, 0)),
            pl.BlockSpec((1, hw), lambda b, s, pt: (0, 0)),
            qk_spec, qk_spec, v_spec,
        ] + [page_spec(j, n_maps, HEAD_DIM) for j in range(npg)] + [page_spec(j, N_HEADS, hw) for j in range(npg)],
        out_specs=v_spec,
        scratch_shapes=[
            pltpu.VMEM((n_maps, 1), F32),
            pltpu.VMEM((n_maps, 1), F32),
            pltpu.VMEM((n_maps, hw), F32),
        ],
    )
    return pl.pallas_call(
        functools.partial(_attn_decode_kernel, pages_per_step=npg, page=page, lam_init=lam_init),
        grid_spec=grid_spec,
        out_shape=jax.ShapeDtypeStruct((rows, N_HEADS, hw), F32),
        compiler_params=_params(2),
        name="attn_decode",
    )(page_table, lam4, g, q, k_new, v_new, *([cache_k] * npg), *([cache_v] * npg))


def _rope_tables(pos):
    half = HEAD_DIM // 2
    inv_freq = jnp.power(ROPE_THETA, -jnp.arange(half, dtype=F32) * (2.0 / HEAD_DIM))
    ang = pos.astype(F32)[:, None] * inv_freq[None, :]
    cos = jnp.cos(ang)
    sin = jnp.sin(ang)
    return jnp.concatenate([cos, cos], axis=-1), jnp.concatenate([-sin, sin], axis=-1)


def kernel(x_prompt, x_sample, state_conv, state_h, cache_k, cache_v, page_table, norm_mix_pre, norm_mix_post, norm_ffn_pre, norm_ffn_post, lru_w_in, lru_b_in, lru_conv_w, lru_conv_b, lru_w_gate_a, lru_b_gate_a, lru_w_gate_x, lru_b_gate_x, lru_a_param, lru_w_out, attn_w_qkv, attn_lambda_q1, attn_lambda_k1, attn_lambda_q2, attn_lambda_k2, attn_subln, attn_w_o, ffn_w_gate, ffn_w_up, ffn_w_down):
    bp, seq, d = x_prompt.shape
    bd, t_dec, _ = x_sample.shape
    assert d == D_MODEL and t_dec == 1
    depth = norm_mix_pre.shape[0]
    n_pages = page_table.shape[1]
    page = cache_k.shape[2]
    past = n_pages * page
    mp = bp * seq

    xp = x_prompt.reshape(mp, d)
    xs = x_sample.reshape(bd, d)
    row = lambda v: v.reshape(1, -1)

    conv_p, h_p, k_p, v_p = [], [], [], []
    conv_s, h_s, k_s, v_s = [], [], [], []
    for i in range(depth):
        j = i // N_MIXERS
        g_pre = row(norm_mix_pre[i])
        g_post = row(norm_mix_post[i])
        if i % N_MIXERS == 0:
            w_in = lru_w_in[j].astype(BF16)
            b_in = row(lru_b_in[j])
            w_out = lru_w_out[j].astype(BF16)
            wa = lru_w_gate_a[j].astype(BF16)
            wx = lru_w_gate_x[j].astype(BF16)
            ba = lru_b_gate_a[j].reshape(LRU_BLOCKS, 1, LRU_BLOCK)
            bx = lru_b_gate_x[j].reshape(LRU_BLOCKS, 1, LRU_BLOCK)
            gate_prm = (lru_conv_w[j], row(lru_conv_b[j]), wa, ba, wx, bx, row(lru_a_param[j]))
            proj_p = _lru_in_proj(xp, g_pre, w_in, b_in, tm=1024, tn=1024)
            hy_p, hl_p = _lru_seq(proj_p, jnp.zeros((bp, SUBLANES, LRU_WIDTH), F32), jnp.zeros((bp, 1, LRU_WIDTH), F32),
                                  *gate_prm, batch=bp, seq=seq, tt=256)
            conv_p.append(proj_p.reshape(bp, seq, 2 * LRU_WIDTH)[:, seq - (CONV_WIDTH - 1):, LRU_WIDTH:])
            h_p.append(hl_p.reshape(bp, LRU_WIDTH))
            xp = _out_proj_residual(hy_p, w_out, g_post, xp, tm=512)
            proj_s = _lru_in_proj(xs, g_pre, w_in, b_in, tm=bd, tn=1024)
            hy_s, hl_s = _lru_step(proj_s, state_conv[j].transpose(1, 0, 2), state_h[j], *gate_prm)
            conv_s.append(jnp.concatenate([state_conv[j][:, 1:], proj_s[:, None, LRU_WIDTH:]], axis=1))
            h_s.append(hl_s)
            xs = _out_proj_residual(hy_s, w_out, g_post, xs, tm=bd)
        else:
            lam_init = 0.8 - 0.6 * math.exp(-0.3 * i)
            w_qkv = attn_w_qkv[j].astype(BF16)
            w_o = attn_w_o[j].astype(BF16)
            lam4 = jnp.stack([attn_lambda_q1[j], attn_lambda_k1[j], attn_lambda_q2[j], attn_lambda_k2[j]])
            g_sub = row(attn_subln[j])
            cos_p, sin_p = _rope_tables(jnp.arange(seq))
            qkv_p, k4_p, v4_p = _qkv_proj(xp, g_pre, w_qkv, cos_p, sin_p, tm=1024, tn=1024)
            o = _attn_prompt(qkv_p, lam4, g_sub, batch=bp, seq=seq, tile=512, lam_init=lam_init)
            k_p.append(k4_p.reshape(bp, seq, 2 * N_HEADS, HEAD_DIM))
            v_p.append(v4_p.reshape(bp, seq, N_HEADS, 2 * HEAD_DIM))
            xp = _out_proj_residual(o, w_o, g_post, xp, tm=512)
            cos_s, sin_s = _rope_tables(jnp.full((bd,), past))
            qkv_s, ks_new, vs_new = _qkv_proj(xs, g_pre, w_qkv, cos_s, sin_s, tm=bd, tn=1024)
            map_major = lambda a: a.reshape(bd, N_HEADS, 2, HEAD_DIM).transpose(0, 2, 1, 3).reshape(bd, 2 * N_HEADS, HEAD_DIM)
            os_ = _attn_decode(map_major(qkv_s[:, :d]), map_major(ks_new), vs_new,
                               cache_k, cache_v, j, page_table, lam4, g_sub, pages_per_step=8, lam_init=lam_init)
            k_s.append(ks_new.reshape(bd, 1, 2 * N_HEADS, HEAD_DIM))
            v_s.append(vs_new.reshape(bd, 1, N_HEADS, 2 * HEAD_DIM))
            xs = _out_proj_residual(os_.reshape(bd, d), w_o, g_post, xs, tm=bd)
        gf_pre = row(norm_ffn_pre[i])
        gf_post = row(norm_ffn_post[i])
        xp = _ffn(xp, gf_pre, ffn_w_gate, ffn_w_up, ffn_w_down, gf_post, i, tm=1024, tf=512)
        xs = _ffn(xs, gf_pre, ffn_w_gate, ffn_w_up, ffn_w_down, gf_post, i, tm=bd, tf=512)
    return (xp.reshape(bp, seq, d), xs.reshape(bd, 1, d), jnp.stack(conv_p), jnp.stack(h_p), jnp.stack(k_p), jnp.stack(v_p),
            jnp.stack(conv_s), jnp.stack(h_s), jnp.stack(k_s), jnp.stack(v_s))
```

```python
import functools
import math

import jax
import jax.numpy as jnp
from jax import lax
from jax.experimental import pallas as pl
from jax.experimental.pallas import tpu as pltpu

D_MODEL = 2048
LRU_WIDTH = D_MODEL
LRU_BLOCK = 256
LRU_BLOCKS = LRU_WIDTH // LRU_BLOCK
CONV_WIDTH = 4
LRU_C = 8.0
HEAD_DIM = 128
N_HEADS = D_MODEL // (2 * HEAD_DIM)
ROPE_THETA = 10000.0
NORM_EPS = 1e-6
SUBLN_EPS = 1e-5
N_MIXERS = 2
Q_SCALE = HEAD_DIM ** -0.5 * math.log2(math.e)

SUBLANES = 8
LANES = 128
VMEM_CAPACITY_BYTES = 64 * 1024 * 1024
VMEM_LIMIT_BYTES = VMEM_CAPACITY_BYTES - 4 * 1024 * 1024

BF16 = jnp.bfloat16
F32 = jnp.float32


def _params(n_grid_dims):
    return pltpu.CompilerParams(
        dimension_semantics=("arbitrary",) * n_grid_dims,
        vmem_limit_bytes=VMEM_LIMIT_BYTES,
    )


def _rms_scale(x, eps):
    return lax.rsqrt(jnp.mean(x * x, axis=-1, keepdims=True) + eps)


def _sigmoid(x):
    return 1.0 / (1.0 + jnp.exp(-x))


def _gelu_tanh(x):
    c = math.sqrt(2.0 / math.pi)
    return 0.5 * x * (1.0 + jnp.tanh(c * (x + 0.044715 * (x * x * x))))


def _softplus(z):
    return jnp.maximum(z, 0.0) + jnp.log1p(jnp.exp(-jnp.abs(z)))


def _norm_once(x_ref, g_ref, h_ref):
    @pl.when(pl.program_id(1) == 0)
    def _():
        x = x_ref[...]
        h_ref[...] = (x * _rms_scale(x, NORM_EPS) * g_ref[...]).astype(BF16)


def _lru_in_kernel(x_ref, g_ref, w_ref, b_ref, o_ref, h_ref, *, n_gate_tiles):
    _norm_once(x_ref, g_ref, h_ref)
    j = pl.program_id(1)
    proj = lambda: jnp.dot(h_ref[...], w_ref[...], preferred_element_type=F32) + b_ref[...]

    @pl.when(j < n_gate_tiles)
    def _():
        o_ref[...] = _gelu_tanh(proj())

    @pl.when(j >= n_gate_tiles)
    def _():
        o_ref[...] = proj()


def _lru_in_proj(x, g, w, b, *, tm, tn):
    m, d = x.shape
    n = w.shape[1]
    return pl.pallas_call(
        functools.partial(_lru_in_kernel, n_gate_tiles=LRU_WIDTH // tn),
        grid=(m // tm, n // tn),
        in_specs=[
            pl.BlockSpec((tm, d), lambda i, j: (i, 0)),
            pl.BlockSpec((1, d), lambda i, j: (0, 0)),
            pl.BlockSpec((d, tn), lambda i, j: (0, j)),
            pl.BlockSpec((1, tn), lambda i, j: (0, j)),
        ],
        out_specs=pl.BlockSpec((tm, tn), lambda i, j: (i, j)),
        out_shape=jax.ShapeDtypeStruct((m, n), F32),
        scratch_shapes=[pltpu.VMEM((tm, d), BF16)],
        compiler_params=_params(2),
        name="lru_in_proj",
    )(x, g, w, b)


def _rope(x, cos2, sin2):
    return x * cos2 + pltpu.roll(x, HEAD_DIM // 2, axis=1) * sin2


def _qkv_kernel(x_ref, g_ref, w_ref, cos_ref, sin_ref, o_ref, k4_ref, v4_ref, h_ref, *, tiles_per_part, heads_per_tile):
    _norm_once(x_ref, g_ref, h_ref)
    j = pl.program_id(1)
    tm = o_ref.shape[0]
    proj = lambda: jnp.dot(h_ref[...], w_ref[...], preferred_element_type=F32)

    def roped(acc, scale):
        cos2 = cos_ref[...]
        sin2 = sin_ref[...]
        parts = []
        for hd in range(heads_per_tile):
            r = _rope(acc[:, hd * HEAD_DIM:(hd + 1) * HEAD_DIM], cos2, sin2)
            parts.append(r if scale is None else r * scale)
        return jnp.concatenate(parts, axis=1)

    @pl.when(j < tiles_per_part)
    def _():
        o_ref[...] = roped(proj(), Q_SCALE)

    @pl.when((j >= tiles_per_part) & (j < 2 * tiles_per_part))
    def _():
        k = roped(proj(), None)
        o_ref[...] = k
        k4_ref[...] = k.reshape(tm, heads_per_tile, HEAD_DIM)

    v_heads = heads_per_tile // 2
    for t in range(tiles_per_part):
        @pl.when(j == 2 * tiles_per_part + t)
        def _(t=t):
            v = proj()
            o_ref[...] = v
            v4_ref[:, t * v_heads:(t + 1) * v_heads, :] = v.reshape(tm, v_heads, 2 * HEAD_DIM)


def _qkv_proj(x, g, w, cos2, sin2, *, tm, tn):
    m, d = x.shape
    n = w.shape[1]
    tpp = d // tn
    hpt = tn // HEAD_DIM
    n_pos_tiles = cos2.shape[0] // tm
    return pl.pallas_call(
        functools.partial(_qkv_kernel, tiles_per_part=tpp, heads_per_tile=hpt),
        grid=(m // tm, n // tn),
        in_specs=[
            pl.BlockSpec((tm, d), lambda i, j: (i, 0), pipeline_mode=pl.Buffered(1)),
            pl.BlockSpec((1, d), lambda i, j: (0, 0)),
            pl.BlockSpec((d, tn), lambda i, j: (0, j)),
            pl.BlockSpec((tm, HEAD_DIM), lambda i, j: (i % n_pos_tiles, 0)),
            pl.BlockSpec((tm, HEAD_DIM), lambda i, j: (i % n_pos_tiles, 0)),
        ],
        out_specs=[
            pl.BlockSpec((tm, tn), lambda i, j: (i, j)),
            pl.BlockSpec((tm, hpt, HEAD_DIM), lambda i, j: (i, jnp.clip(j - tpp, 0, tpp - 1), 0)),
            pl.BlockSpec((tm, N_HEADS, 2 * HEAD_DIM), lambda i, j: (i, 0, 0)),
        ],
        out_shape=[
            jax.ShapeDtypeStruct((m, n), F32),
            jax.ShapeDtypeStruct((m, 2 * N_HEADS, HEAD_DIM), F32),
            jax.ShapeDtypeStruct((m, N_HEADS, 2 * HEAD_DIM), F32),
        ],
        scratch_shapes=[pltpu.VMEM((tm, d), BF16)],
        compiler_params=_params(2),
        name="qkv_proj",
    )(x, g, w, cos2, sin2)


def _out_proj_kernel(a_ref, w_ref, g_ref, x_ref, o_ref):
    y = jnp.dot(a_ref[...].astype(BF16), w_ref[...], preferred_element_type=F32)
    o_ref[...] = x_ref[...] + y * _rms_scale(y, NORM_EPS) * g_ref[...]


def _out_proj_residual(a, w, g, x, *, tm):
    m, k = a.shape
    d = w.shape[1]
    return pl.pallas_call(
        _out_proj_kernel,
        grid=(m // tm,),
        in_specs=[
            pl.BlockSpec((tm, k), lambda i: (i, 0)),
            pl.BlockSpec((k, d), lambda i: (0, 0)),
            pl.BlockSpec((1, d), lambda i: (0, 0)),
            pl.BlockSpec((tm, d), lambda i: (i, 0)),
        ],
        out_specs=pl.BlockSpec((tm, d), lambda i: (i, 0)),
        out_shape=jax.ShapeDtypeStruct((m, d), F32),
        compiler_params=_params(1),
        name="out_proj_residual",
    )(a, w, g, x)


def _ffn_kernel(x_ref, gpre_ref, wg_ref, wu_ref, wd_ref, gpost_ref, o_ref, h_ref):
    f = pl.program_id(1)

    @pl.when(f == 0)
    def _():
        x = x_ref[...]
        h_ref[...] = (x * _rms_scale(x, NORM_EPS) * gpre_ref[...]).astype(BF16)
        o_ref[...] = jnp.zeros(o_ref.shape, F32)

    h = h_ref[...]
    gate = jnp.dot(h, wg_ref[...].astype(BF16), preferred_element_type=F32)
    up = jnp.dot(h, wu_ref[...].astype(BF16), preferred_element_type=F32)
    act = (gate * _sigmoid(gate) * up).astype(BF16)
    o_ref[...] += jnp.dot(act, wd_ref[...].astype(BF16), preferred_element_type=F32)

    @pl.when(f == pl.num_programs(1) - 1)
    def _():
        y = o_ref[...]
        o_ref[...] = x_ref[...] + y * _rms_scale(y, NORM_EPS) * gpost_ref[...]


def _ffn(x, gpre, wg, wu, wd, gpost, layer, *, tm, tf):
    m, d = x.shape
    dff = wg.shape[2]
    rows_once = dict(pipeline_mode=pl.Buffered(1))
    return pl.pallas_call(
        _ffn_kernel,
        grid=(m // tm, dff // tf),
        in_specs=[
            pl.BlockSpec((tm, d), lambda i, f: (i, 0), **rows_once),
            pl.BlockSpec((1, d), lambda i, f: (0, 0)),
            pl.BlockSpec((None, d, tf), lambda i, f: (layer, 0, f)),
            pl.BlockSpec((None, d, tf), lambda i, f: (layer, 0, f)),
            pl.BlockSpec((None, tf, d), lambda i, f: (layer, f, 0)),
            pl.BlockSpec((1, d), lambda i, f: (0, 0)),
        ],
        out_specs=pl.BlockSpec((tm, d), lambda i, f: (i, 0), **rows_once),
        out_shape=jax.ShapeDtypeStruct((m, d), F32),
        scratch_shapes=[pltpu.VMEM((tm, d), BF16)],
        compiler_params=_params(2),
        name="ffn",
    )(x, gpre, wg, wu, wd, gpost)


def _lru_gates(xc, wa, ba, wx, bx, sp):
    xcb = xc.astype(BF16)
    gate_a = _sigmoid(jnp.dot(xcb, wa, preferred_element_type=F32) + ba)
    gate_x = _sigmoid(jnp.dot(xcb, wx, preferred_element_type=F32) + bx)
    log_a = (-LRU_C * gate_a) * sp
    a = jnp.exp(log_a)
    y = -jnp.tanh(log_a) * (a * a + 1.0)
    b = jnp.where(y > 0.0, y * lax.rsqrt(y), 0.0) * (gate_x * xc)
    return a, b


def _lru_seq_kernel(xr_ref, yg_ref, conv0_ref, h0_ref, cw_ref, cb_ref, wa_ref, ba_ref, wx_ref, bx_ref, ap_ref,
                    o_ref, hlast_ref, xe_ref, h_ref, a_ref, b_ref, *, tt):
    ti = pl.program_id(1)
    pad = SUBLANES

    @pl.when(ti == 0)
    def _():
        xe_ref[0:pad, :] = conv0_ref[0]
        h_ref[...] = h0_ref[0]

    xe_ref[pad:pad + tt, :] = xr_ref[...]
    sp = _softplus(-ap_ref[...])
    groups = tt // SUBLANES
    rowmod = lax.broadcasted_iota(jnp.int32, (groups, SUBLANES, LRU_BLOCK), 1)

    for n in range(LRU_BLOCKS):
        cs = slice(n * LRU_BLOCK, (n + 1) * LRU_BLOCK)
        xg = xe_ref[:, cs].reshape(groups + 1, SUBLANES, LRU_BLOCK)
        xc = cb_ref[:, cs] + xg[1:].reshape(tt, LRU_BLOCK) * cw_ref[CONV_WIDTH - 1:CONV_WIDTH, cs]
        for k in range(1, CONV_WIDTH):
            rot = pltpu.roll(xg, k, axis=1)
            shifted = jnp.where(rowmod >= k, rot[1:], rot[:-1]).reshape(tt, LRU_BLOCK)
            xc = xc + shifted * cw_ref[CONV_WIDTH - 1 - k:CONV_WIDTH - k, cs]
        a, b = _lru_gates(xc, wa_ref[n], ba_ref[n], wx_ref[n], bx_ref[n], sp[:, cs])
        a = a.reshape(tt // SUBLANES, SUBLANES, LRU_BLOCK)
        b = b.reshape(tt // SUBLANES, SUBLANES, LRU_BLOCK)
        d = 1
        while d < SUBLANES:
            valid = rowmod >= d
            a_sh = jnp.where(valid, pltpu.roll(a, d, axis=1), 1.0)
            b_sh = jnp.where(valid, pltpu.roll(b, d, axis=1), 0.0)
            b = a * b_sh + b
            a = a * a_sh
            d *= 2
        a_ref[:, cs] = a.reshape(tt, LRU_BLOCK)
        b_ref[:, cs] = b.reshape(tt, LRU_BLOCK)

    def group(gi, carry):
        r = pl.multiple_of(gi * SUBLANES, SUBLANES)
        h = a_ref[pl.ds(r, SUBLANES), :] * carry + b_ref[pl.ds(r, SUBLANES), :]
        b_ref[pl.ds(r, SUBLANES), :] = h
        return h[SUBLANES - 1:SUBLANES, :]

    carry = lax.fori_loop(0, tt // SUBLANES, group, h_ref[...], unroll=4)
    h_ref[...] = carry
    hlast_ref[0] = carry
    o_ref[...] = (b_ref[...] * yg_ref[...]).astype(o_ref.dtype)
    xe_ref[0:pad, :] = xe_ref[tt:tt + pad, :]


def _lru_seq(proj, conv0, h0, cw, cb, wa, ba, wx, bx, ap, *, batch, seq, tt):
    w = LRU_WIDTH
    nt = seq // tt
    const2 = lambda b, t: (0, 0)
    const3 = lambda b, t: (0, 0, 0)
    return pl.pallas_call(
        functools.partial(_lru_seq_kernel, tt=tt),
        grid=(batch, nt),
        in_specs=[
            pl.BlockSpec((tt, w), lambda b, t: (b * nt + t, 1)),
            pl.BlockSpec((tt, w), lambda b, t: (b * nt + t, 0)),
            pl.BlockSpec((1, SUBLANES, w), lambda b, t: (b, 0, 0)),
            pl.BlockSpec((1, 1, w), lambda b, t: (b, 0, 0)),
            pl.BlockSpec((CONV_WIDTH, w), const2),
            pl.BlockSpec((1, w), const2),
            pl.BlockSpec((LRU_BLOCKS, LRU_BLOCK, LRU_BLOCK), const3),
            pl.BlockSpec((LRU_BLOCKS, 1, LRU_BLOCK), const3),
            pl.BlockSpec((LRU_BLOCKS, LRU_BLOCK, LRU_BLOCK), const3),
            pl.BlockSpec((LRU_BLOCKS, 1, LRU_BLOCK), const3),
            pl.BlockSpec((1, w), const2),
        ],
        out_specs=[
            pl.BlockSpec((tt, w), lambda b, t: (b * nt + t, 0)),
            pl.BlockSpec((1, 1, w), lambda b, t: (b, 0, 0)),
        ],
        out_shape=[
            jax.ShapeDtypeStruct((batch * seq, w), BF16),
            jax.ShapeDtypeStruct((batch, 1, w), F32),
        ],
        scratch_shapes=[
            pltpu.VMEM((tt + SUBLANES, w), F32),
            pltpu.VMEM((1, w), F32),
            pltpu.VMEM((tt, w), F32),
            pltpu.VMEM((tt, w), F32),
        ],
        compiler_params=_params(2),
        name="lru_seq",
    )(proj, proj, conv0, h0, cw, cb, wa, ba, wx, bx, ap)


def _lru_step_kernel(xr_ref, yg_ref, conv_ref, h0_ref, cw_ref, cb_ref, wa_ref, ba_ref, wx_ref, bx_ref, ap_ref,
                     o_ref, hnew_ref):
    sp = _softplus(-ap_ref[...])
    for n in range(LRU_BLOCKS):
        cs = slice(n * LRU_BLOCK, (n + 1) * LRU_BLOCK)
        xc = cb_ref[:, cs] + conv_ref[0, :, cs] * cw_ref[0:1, cs]
        for tap in range(1, CONV_WIDTH - 1):
            xc = xc + conv_ref[tap, :, cs] * cw_ref[tap:tap + 1, cs]
        xc = xc + xr_ref[:, cs] * cw_ref[CONV_WIDTH - 1:CONV_WIDTH, cs]
        a, b = _lru_gates(xc, wa_ref[n], ba_ref[n], wx_ref[n], bx_ref[n], sp[:, cs])
        h = a * h0_ref[:, cs] + b
        hnew_ref[:, cs] = h
        o_ref[:, cs] = h * yg_ref[:, cs]


def _lru_step(proj, conv_t, h0, cw, cb, wa, ba, wx, bx, ap):
    m = proj.shape[0]
    w = LRU_WIDTH
    c2 = lambda i: (0, 0)
    c3 = lambda i: (0, 0, 0)
    return pl.pallas_call(
        _lru_step_kernel,
        grid=(1,),
        in_specs=[
            pl.BlockSpec((m, w), lambda i: (0, 1)),
            pl.BlockSpec((m, w), lambda i: (0, 0)),
            pl.BlockSpec((CONV_WIDTH - 1, m, w), c3),
            pl.BlockSpec((m, w), c2),
            pl.BlockSpec((CONV_WIDTH, w), c2),
            pl.BlockSpec((1, w), c2),
            pl.BlockSpec((LRU_BLOCKS, LRU_BLOCK, LRU_BLOCK), c3),
            pl.BlockSpec((LRU_BLOCKS, 1, LRU_BLOCK), c3),
            pl.BlockSpec((LRU_BLOCKS, LRU_BLOCK, LRU_BLOCK), c3),
            pl.BlockSpec((LRU_BLOCKS, 1, LRU_BLOCK), c3),
            pl.BlockSpec((1, w), c2),
        ],
        out_specs=[pl.BlockSpec((m, w), c2), pl.BlockSpec((m, w), c2)],
        out_shape=[jax.ShapeDtypeStruct((m, w), F32), jax.ShapeDtypeStruct((m, w), F32)],
        compiler_params=_params(1),
        name="lru_step",
    )(proj, proj, conv_t, h0, cw, cb, wa, ba, wx, bx, ap)


def _lambda(lam_ref, lam_init):
    s1 = jnp.sum(lam_ref[0:1, :] * lam_ref[1:2, :], axis=-1, keepdims=True)
    s2 = jnp.sum(lam_ref[2:3, :] * lam_ref[3:4, :], axis=-1, keepdims=True)
    return jnp.exp(s1) - jnp.exp(s2) + lam_init


def _subln(o, g, lam_init):
    return o * _rms_scale(o, SUBLN_EPS) * g * (1.0 - lam_init)


def _nt_dot(a, b):
    return lax.dot_general(a, b, (((1,), (1,)), ((), ())), preferred_element_type=F32)


def _attn_prompt_kernel(lam_ref, g_ref, q_ref, k_ref, v_ref, o_ref, qb_ref, kb_ref, vb_ref, m_ref, l_ref, acc_ref,
                        *, tile, lam_init):
    maps = range(2)
    hs = [slice(s * HEAD_DIM, (s + 1) * HEAD_DIM) for s in maps]
    seq = q_ref.shape[0]
    half = tile // 2
    qb_ref[...] = q_ref[...].astype(BF16)
    kb_ref[...] = k_ref[...].astype(BF16)
    vb_ref[...] = v_ref[...].astype(BF16)
    lam = _lambda(lam_ref, lam_init)

    def lane_groups(x):
        return [x[:, j * LANES:(j + 1) * LANES] for j in range(x.shape[1] // LANES)]

    def scores(q0, rows, k0, klen, diagonal):
        kc = kb_ref[pl.ds(k0, klen), :]
        sc = [_nt_dot(qb_ref[q0:q0 + rows, hs[s]], kc[:, hs[s]]) for s in maps]
        if diagonal:
            keep = (lax.broadcasted_iota(jnp.int32, (rows, klen), 1)
                    <= lax.broadcasted_iota(jnp.int32, (rows, klen), 0))
            sc = [jnp.where(keep, x, -jnp.inf) for x in sc]
        return sc

    for qi in range(seq // tile):
        base = qi * tile

        def max_block(q0, rows, k0, klen, diagonal, base=base):
            sc = scores(q0, rows, k0, klen, diagonal)
            r = slice(q0 - base, q0 - base + rows)
            m_old = [m_ref[s, r] for s in maps]
            for s in maps:
                m_ref[s, r] = functools.reduce(jnp.maximum, lane_groups(sc[s]), m_old[s])

        def sum_block(q0, rows, k0, klen, diagonal, base=base):
            sc = scores(q0, rows, k0, klen, diagonal)
            vc = vb_ref[pl.ds(k0, klen), :]
            r = slice(q0 - base, q0 - base + rows)
            m_row = [m_ref[s, r] for s in maps]
            l_old = [l_ref[s, r] for s in maps]
            acc_old = [acc_ref[s, r] for s in maps]
            p = [[jnp.exp2(x - m_row[s]) for x in lane_groups(sc[s])] for s in maps]
            for s in maps:
                l_ref[s, r] = functools.reduce(jnp.add, p[s], l_old[s])
                pb = jnp.concatenate(p[s], axis=1).astype(BF16)
                acc_ref[s, r] = acc_old[s] + jnp.dot(pb, vc, preferred_element_type=F32)

        def sweep(block, qi=qi, base=base):
            def left(c, carry):
                block(base, tile, pl.multiple_of(c * tile, tile), tile, False)
                return carry
            if qi:
                lax.fori_loop(0, qi, left, 0)
            block(base, half, base, half, True)
            block(base + half, half, base, half, False)
            block(base + half, half, base + half, half, True)

        m_ref[...] = jnp.full(m_ref.shape, -jnp.inf, F32)
        sweep(max_block)
        for s in maps:
            m_ref[s] = jnp.broadcast_to(jnp.max(m_ref[s], axis=-1, keepdims=True), (tile, LANES))
        l_ref[...] = jnp.zeros(l_ref.shape, F32)
        acc_ref[...] = jnp.zeros(acc_ref.shape, F32)
        sweep(sum_block)
        inv = [1.0 / jnp.sum(l_ref[s], axis=-1, keepdims=True) for s in maps]
        o = acc_ref[0] * inv[0] - lam * (acc_ref[1] * inv[1])
        o_ref[base:base + tile, :] = _subln(o, g_ref[...], lam_init).astype(o_ref.dtype)


def _attn_prompt(qkv, lam4, g, *, batch, seq, tile, lam_init):
    hw = 2 * HEAD_DIM
    part = D_MODEL // hw
    return pl.pallas_call(
        functools.partial(_attn_prompt_kernel, tile=tile, lam_init=lam_init),
        grid=(batch, N_HEADS),
        in_specs=[
            pl.BlockSpec((4, HEAD_DIM), lambda b, h: (0, 0)),
            pl.BlockSpec((1, hw), lambda b, h: (0, 0)),
            pl.BlockSpec((seq, hw), lambda b, h: (b, h)),
            pl.BlockSpec((seq, hw), lambda b, h: (b, part + h)),
            pl.BlockSpec((seq, hw), lambda b, h: (b, 2 * part + h)),
        ],
        out_specs=pl.BlockSpec((seq, hw), lambda b, h: (b, h)),
        out_shape=jax.ShapeDtypeStruct((batch * seq, D_MODEL), BF16),
        scratch_shapes=[
            pltpu.VMEM((seq, hw), BF16),
            pltpu.VMEM((seq, hw), BF16),
            pltpu.VMEM((seq, hw), BF16),
            pltpu.VMEM((2, tile, LANES), F32),
            pltpu.VMEM((2, tile, LANES), F32),
            pltpu.VMEM((2, tile, hw), F32),
        ],
        compiler_params=_params(2),
        name="attn_prompt",
    )(lam4, g, qkv, qkv, qkv)


def _attn_decode_kernel(pt_ref, lam_ref, g_ref, q_ref, knew_ref, vnew_ref, *rest, pages_per_step, page, lam_init):
    del pt_ref
    npg = pages_per_step
    k_refs = rest[:npg]
    v_refs = rest[npg:2 * npg]
    o_ref, m_ref, l_ref, acc_ref = rest[2 * npg:]
    step = pl.program_id(1)
    n_maps = 2 * N_HEADS
    cols = page * N_HEADS

    @pl.when(step == 0)
    def _():
        m_ref[...] = jnp.full(m_ref.shape, -jnp.inf, F32)
        l_ref[...] = jnp.zeros(l_ref.shape, F32)
        acc_ref[...] = jnp.zeros(acc_ref.shape, F32)

    q = q_ref[0]
    qb = q.astype(BF16)
    row = lax.broadcasted_iota(jnp.int32, (n_maps, cols), 0)
    col = lax.broadcasted_iota(jnp.int32, (n_maps, cols), 1)
    first_map = row < N_HEADS
    own = (col & (N_HEADS - 1)) == (row & (N_HEADS - 1))

    scores = []
    for j in range(npg):
        k_even = k_refs[j][:, pl.ds(0, N_HEADS, stride=2), :].reshape(cols, HEAD_DIM).astype(BF16)
        k_odd = k_refs[j][:, pl.ds(1, N_HEADS, stride=2), :].reshape(cols, HEAD_DIM).astype(BF16)
        sc = jnp.where(first_map, _nt_dot(qb, k_even), _nt_dot(qb, k_odd))
        scores.append(jnp.where(own, sc, -jnp.inf))
    sc = jnp.concatenate(scores, axis=1) if npg > 1 else scores[0]
    m_old = m_ref[...]
    m_new = jnp.maximum(m_old, jnp.max(sc, axis=-1, keepdims=True))
    alpha = jnp.exp2(m_old - m_new)
    p = jnp.exp2(sc - m_new)
    l_ref[...] = alpha * l_ref[...] + jnp.sum(p, axis=-1, keepdims=True)
    pb = p.astype(BF16)
    pv = None
    for j in range(npg):
        vj = v_refs[j][...].reshape(cols, 2 * HEAD_DIM).astype(BF16)
        part = jnp.dot(pb[:, j * cols:(j + 1) * cols], vj, preferred_element_type=F32)
        pv = part if pv is None else pv + part
    acc_ref[...] = alpha * acc_ref[...] + pv
    m_ref[...] = m_new

    @pl.when(step == pl.num_programs(1) - 1)
    def _():
        s_new = jnp.sum(q * knew_ref[0], axis=-1, keepdims=True)
        m_old2 = m_ref[...]
        m_fin = jnp.maximum(m_old2, s_new)
        alpha2 = jnp.exp2(m_old2 - m_fin)
        p_new = jnp.exp2(s_new - m_fin)
        l_fin = alpha2 * l_ref[...] + p_new
        v_new = vnew_ref[0]
        acc = (alpha2 * acc_ref[...] + p_new * jnp.concatenate([v_new, v_new], axis=0)) / l_fin
        lam = _lambda(lam_ref, lam_init)
        o = acc[0:N_HEADS, :] - lam * acc[N_HEADS:n_maps, :]
        o_ref[0] = _subln(o, g_ref[...], lam_init)


def _attn_decode(q, k_new, v_new, cache_k, cache_v, layer, page_table, lam4, g, *, pages_per_step, lam_init):
    rows = q.shape[0]
    n_pages = page_table.shape[1]
    page = cache_k.shape[2]
    npg = pages_per_step
    hw = 2 * HEAD_DIM
    n_maps = 2 * N_HEADS

    def page_spec(j, heads, width):
        return pl.BlockSpec((None, None, page, heads, width),
                            lambda b, s, pt: (layer, pt[b, s * npg + j], 0, 0, 0))

    qk_spec = pl.BlockSpec((1, n_maps, HEAD_DIM), lambda b, s, pt: (b, 0, 0))
    v_spec = pl.BlockSpec((1, N_HEADS, hw), lambda b, s, pt: (b, 0, 0))
    grid_spec = pltpu.PrefetchScalarGridSpec(
        num_scalar_prefetch=1,
        grid=(rows, n_pages // npg),
        in_specs=[
            pl.BlockSpec((4, HEAD_DIM), lambda b, s, pt: (0, 0)),
            pl.BlockSpec((1, hw), lambda b, s, pt: (0, 0)),
            qk_spec, qk_spec, v_spec,
        ] + [page_spec(j, n_maps, HEAD_DIM) for j in range(npg)] + [page_spec(j, N_HEADS, hw) for j in range(npg)],
        out_specs=v_spec,
        scratch_shapes=[
            pltpu.VMEM((n_maps, 1), F32),
            pltpu.VMEM((n_maps, 1), F32),
            pltpu.VMEM((n_maps, hw), F32),
        ],
    )
    return pl.pallas_call(
        functools.partial(_attn_decode_kernel, pages_per_step=npg, page=page, lam_init=lam_init),
        grid_spec=grid_spec,
        out_shape=jax.ShapeDtypeStruct((rows, N_HEADS, hw), F32),
        compiler_params=_params(2),
        name="attn_decode",
    )(page_table, lam4, g, q, k_new, v_new, *([cache_k] * npg), *([cache_v] * npg))


def _rope_tables(pos):
    half = HEAD_DIM // 2
    inv_freq = jnp.power(ROPE_THETA, -jnp.arange(half, dtype=F32) * (2.0 / HEAD_DIM))
    ang = pos.astype(F32)[:, None] * inv_freq[None, :]
    cos = jnp.cos(ang)
    sin = jnp.sin(ang)
    return jnp.concatenate([cos, cos], axis=-1), jnp.concatenate([-sin, sin], axis=-1)


def kernel(x_prompt, x_sample, state_conv, state_h, cache_k, cache_v, page_table, norm_mix_pre, norm_mix_post, norm_ffn_pre, norm_ffn_post, lru_w_in, lru_b_in, lru_conv_w, lru_conv_b, lru_w_gate_a, lru_b_gate_a, lru_w_gate_x, lru_b_gate_x, lru_a_param, lru_w_out, attn_w_qkv, attn_lambda_q1, attn_lambda_k1, attn_lambda_q2, attn_lambda_k2, attn_subln, attn_w_o, ffn_w_gate, ffn_w_up, ffn_w_down):
    bp, seq, d = x_prompt.shape
    bd, t_dec, _ = x_sample.shape
    assert d == D_MODEL and t_dec == 1
    depth = norm_mix_pre.shape[0]
    n_pages = page_table.shape[1]
    page = cache_k.shape[2]
    past = n_pages * page
    mp = bp * seq

    xp = x_prompt.reshape(mp, d)
    xs = x_sample.reshape(bd, d)
    row = lambda v: v.reshape(1, -1)

    conv_p, h_p, k_p, v_p = [], [], [], []
    conv_s, h_s, k_s, v_s = [], [], [], []
    for i in range(depth):
        j = i // N_MIXERS
        g_pre = row(norm_mix_pre[i])
        g_post = row(norm_mix_post[i])
        if i % N_MIXERS == 0:
            w_in = lru_w_in[j].astype(BF16)
            b_in = row(lru_b_in[j])
            w_out = lru_w_out[j].astype(BF16)
            wa = lru_w_gate_a[j].astype(BF16)
            wx = lru_w_gate_x[j].astype(BF16)
            ba = lru_b_gate_a[j].reshape(LRU_BLOCKS, 1, LRU_BLOCK)
            bx = lru_b_gate_x[j].reshape(LRU_BLOCKS, 1, LRU_BLOCK)
            gate_prm = (lru_conv_w[j], row(lru_conv_b[j]), wa, ba, wx, bx, row(lru_a_param[j]))
            proj_p = _lru_in_proj(xp, g_pre, w_in, b_in, tm=1024, tn=1024)
            hy_p, hl_p = _lru_seq(proj_p, jnp.zeros((bp, SUBLANES, LRU_WIDTH), F32), jnp.zeros((bp, 1, LRU_WIDTH), F32),
                                  *gate_prm, batch=bp, seq=seq, tt=256)
            conv_p.append(proj_p.reshape(bp, seq, 2 * LRU_WIDTH)[:, seq - (CONV_WIDTH - 1):, LRU_WIDTH:])
            h_p.append(hl_p.reshape(bp, LRU_WIDTH))
            xp = _out_proj_residual(hy_p, w_out, g_post, xp, tm=512)
            proj_s = _lru_in_proj(xs, g_pre, w_in, b_in, tm=bd, tn=1024)
            hy_s, hl_s = _lru_step(proj_s, state_conv[j].transpose(1, 0, 2), state_h[j], *gate_prm)
            conv_s.append(jnp.concatenate([state_conv[j][:, 1:], proj_s[:, None, LRU_WIDTH:]], axis=1))
            h_s.append(hl_s)
            xs = _out_proj_residual(hy_s, w_out, g_post, xs, tm=bd)
        else:
            lam_init = 0.8 - 0.6 * math.exp(-0.3 * i)
            w_qkv = attn_w_qkv[j].astype(BF16)
            w_o = attn_w_o[j].astype(BF16)
            lam4 = jnp.stack([attn_lambda_q1[j], attn_lambda_k1[j], attn_lambda_q2[j], attn_lambda_k2[j]])
            g_sub = row(attn_subln[j])
            cos_p, sin_p = _rope_tables(jnp.arange(seq))
            qkv_p, k4_p, v4_p = _qkv_proj(xp, g_pre, w_qkv, cos_p, sin_p, tm=1024, tn=1024)
            o = _attn_prompt(qkv_p, lam4, g_sub, batch=bp, seq=seq, tile=512, lam_init=lam_init)
            k_p.append(k4_p.reshape(bp, seq, 2 * N_HEADS, HEAD_DIM))
            v_p.append(v4_p.reshape(bp, seq, N_HEADS, 2 * HEAD_DIM))
            xp = _out_proj_residual(o, w_o, g_post, xp, tm=512)
            cos_s, sin_s = _rope_tables(jnp.full((bd,), past))
            qkv_s, ks_new, vs_new = _qkv_proj(xs, g_pre, w_qkv, cos_s, sin_s, tm=bd, tn=1024)
            map_major = lambda a: a.reshape(bd, N_HEADS, 2, HEAD_DIM).transpose(0, 2, 1, 3).reshape(bd, 2 * N_HEADS, HEAD_DIM)
            os_ = _attn_decode(map_major(qkv_s[:, :d]), map_major(ks_new), vs_new,
                               cache_k, cache_v, j, page_table, lam4, g_sub, pages_per_step=8, lam_init=lam_init)
            k_s.append(ks_new.reshape(bd, 1, 2 * N_HEADS, HEAD_DIM))
            v_s.append(vs_new.reshape(bd, 1, N_HEADS, 2 * HEAD_DIM))
            xs = _out_proj_residual(os_.reshape(bd, d), w_o, g_post, xs, tm=bd)
        gf_pre = row(norm_ffn_pre[i])
        gf_post = row(norm_ffn_post[i])
        xp = _ffn(xp, gf_pre, ffn_w_gate, ffn_w_up, ffn_w_down, gf_post, i, tm=1024, tf=512)
        xs = _ffn(xs, gf_pre, ffn_w_gate, ffn_w_up, ffn_w_down, gf_post, i, tm=bd, tf=512)
    return (xp.reshape(bp, seq, d), xs.reshape(bd, 1, d), jnp.stack(conv_p), jnp.stack(h_p), jnp.stack(k_p), jnp.stack(v_p),
            jnp.stack(conv_s), jnp.stack(h_s), jnp.stack(k_s), jnp.stack(v_s))
```

```python
import functools
import math

import jax
import jax.numpy as jnp
from jax import lax
from jax.experimental import pallas as pl
from jax.experimental.pallas import tpu as pltpu

D_MODEL = 2048
LRU_WIDTH = D_MODEL
LRU_BLOCK = 256
LRU_BLOCKS = LRU_WIDTH // LRU_BLOCK
CONV_WIDTH = 4
LRU_C = 8.0
HEAD_DIM = 128
N_HEADS = D_MODEL // (2 * HEAD_DIM)
ROPE_THETA = 10000.0
NORM_EPS = 1e-6
SUBLN_EPS = 1e-5
N_MIXERS = 2
Q_SCALE = HEAD_DIM ** -0.5 * math.log2(math.e)

SUBLANES = 8
LANES = 128
VMEM_CAPACITY_BYTES = 64 * 1024 * 1024
VMEM_LIMIT_BYTES = VMEM_CAPACITY_BYTES - 4 * 1024 * 1024

BF16 = jnp.bfloat16
F32 = jnp.float32


def _params(n_grid_dims):
    return pltpu.CompilerParams(
        dimension_semantics=("arbitrary",) * n_grid_dims,
        vmem_limit_bytes=VMEM_LIMIT_BYTES,
    )


def _rms_scale(x, eps):
    return lax.rsqrt(jnp.mean(x * x, axis=-1, keepdims=True) + eps)


def _sigmoid(x):
    return 1.0 / (1.0 + jnp.exp(-x))


def _gelu_tanh(x):
    c = math.sqrt(2.0 / math.pi)
    return 0.5 * x * (1.0 + jnp.tanh(c * (x + 0.044715 * (x * x * x))))


def _softplus(z):
    return jnp.maximum(z, 0.0) + jnp.log1p(jnp.exp(-jnp.abs(z)))


def _norm_once(x_ref, g_ref, h_ref):
    @pl.when(pl.program_id(1) == 0)
    def _():
        x = x_ref[...]
        h_ref[...] = (x * _rms_scale(x, NORM_EPS) * g_ref[...]).astype(BF16)


def _lru_in_kernel(x_ref, g_ref, w_ref, b_ref, o_ref, h_ref, *, n_gate_tiles):
    _norm_once(x_ref, g_ref, h_ref)
    j = pl.program_id(1)
    proj = lambda: jnp.dot(h_ref[...], w_ref[...], preferred_element_type=F32) + b_ref[...]

    @pl.when(j < n_gate_tiles)
    def _():
        o_ref[...] = _gelu_tanh(proj())

    @pl.when(j >= n_gate_tiles)
    def _():
        o_ref[...] = proj()


def _lru_in_proj(x, g, w, b, *, tm, tn):
    m, d = x.shape
    n = w.shape[1]
    return pl.pallas_call(
        functools.partial(_lru_in_kernel, n_gate_tiles=LRU_WIDTH // tn),
        grid=(m // tm, n // tn),
        in_specs=[
            pl.BlockSpec((tm, d), lambda i, j: (i, 0)),
            pl.BlockSpec((1, d), lambda i, j: (0, 0)),
            pl.BlockSpec((d, tn), lambda i, j: (0, j)),
            pl.BlockSpec((1, tn), lambda i, j: (0, j)),
        ],
        out_specs=pl.BlockSpec((tm, tn), lambda i, j: (i, j)),
        out_shape=jax.ShapeDtypeStruct((m, n), F32),
        scratch_shapes=[pltpu.VMEM((tm, d), BF16)],
        compiler_params=_params(2),
        name="lru_in_proj",
    )(x, g, w, b)


def _rope(x, cos2, sin2):
    return x * cos2 + pltpu.roll(x, HEAD_DIM // 2, axis=1) * sin2


def _qkv_kernel(x_ref, g_ref, w_ref, cos_ref, sin_ref, o_ref, k4_ref, v4_ref, h_ref, *, tiles_per_part, heads_per_tile):
    _norm_once(x_ref, g_ref, h_ref)
    j = pl.program_id(1)
    tm = o_ref.shape[0]
    proj = lambda: jnp.dot(h_ref[...], w_ref[...], preferred_element_type=F32)

    def roped(acc, scale):
        cos2 = cos_ref[...]
        sin2 = sin_ref[...]
        parts = []
        for hd in range(heads_per_tile):
            r = _rope(acc[:, hd * HEAD_DIM:(hd + 1) * HEAD_DIM], cos2, sin2)
            parts.append(r if scale is None else r * scale)
        return jnp.concatenate(parts, axis=1)

    @pl.when(j < tiles_per_part)
    def _():
        o_ref[...] = roped(proj(), Q_SCALE).astype(o_ref.dtype)

    @pl.when((j >= tiles_per_part) & (j < 2 * tiles_per_part))
    def _():
        k = roped(proj(), None)
        o_ref[...] = k.astype(o_ref.dtype)
        k4_ref[...] = k.reshape(tm, heads_per_tile, HEAD_DIM)

    v_heads = heads_per_tile // 2
    for t in range(tiles_per_part):
        @pl.when(j == 2 * tiles_per_part + t)
        def _(t=t):
            v = proj()
            o_ref[...] = v.astype(o_ref.dtype)
            v4_ref[:, t * v_heads:(t + 1) * v_heads, :] = v.reshape(tm, v_heads, 2 * HEAD_DIM)


def _qkv_proj(x, g, w, cos2, sin2, *, tm, tn):
    m, d = x.shape
    n = w.shape[1]
    tpp = d // tn
    hpt = tn // HEAD_DIM
    n_pos_tiles = cos2.shape[0] // tm
    return pl.pallas_call(
        functools.partial(_qkv_kernel, tiles_per_part=tpp, heads_per_tile=hpt),
        grid=(m // tm, n // tn),
        in_specs=[
            pl.BlockSpec((tm, d), lambda i, j: (i, 0), pipeline_mode=pl.Buffered(1)),
            pl.BlockSpec((1, d), lambda i, j: (0, 0)),
            pl.BlockSpec((d, tn), lambda i, j: (0, j)),
            pl.BlockSpec((tm, HEAD_DIM), lambda i, j: (i % n_pos_tiles, 0)),
            pl.BlockSpec((tm, HEAD_DIM), lambda i, j: (i % n_pos_tiles, 0)),
        ],
        out_specs=[
            pl.BlockSpec((tm, tn), lambda i, j: (i, j)),
            pl.BlockSpec((tm, hpt, HEAD_DIM), lambda i, j: (i, jnp.clip(j - tpp, 0, tpp - 1), 0)),
            pl.BlockSpec((tm, N_HEADS, 2 * HEAD_DIM), lambda i, j: (i, 0, 0)),
        ],
        out_shape=[
            jax.ShapeDtypeStruct((m, n), BF16),
            jax.ShapeDtypeStruct((m, 2 * N_HEADS, HEAD_DIM), F32),
            jax.ShapeDtypeStruct((m, N_HEADS, 2 * HEAD_DIM), F32),
        ],
        scratch_shapes=[pltpu.VMEM((tm, d), BF16)],
        compiler_params=_params(2),
        name="qkv_proj",
    )(x, g, w, cos2, sin2)


def _out_proj_kernel(a_ref, w_ref, g_ref, x_ref, o_ref, wb_ref):
    @pl.when(pl.program_id(0) == 0)
    def _():
        wb_ref[...] = w_ref[...].astype(BF16)

    y = jnp.dot(a_ref[...].astype(BF16), wb_ref[...], preferred_element_type=F32)
    o_ref[...] = x_ref[...] + y * _rms_scale(y, NORM_EPS) * g_ref[...]


def _out_proj_residual(a, w, g, x, *, tm):
    m, k = a.shape
    d = w.shape[1]
    return pl.pallas_call(
        _out_proj_kernel,
        grid=(m // tm,),
        in_specs=[
            pl.BlockSpec((tm, k), lambda i: (i, 0)),
            pl.BlockSpec((k, d), lambda i: (0, 0), pipeline_mode=pl.Buffered(1)),
            pl.BlockSpec((1, d), lambda i: (0, 0)),
            pl.BlockSpec((tm, d), lambda i: (i, 0)),
        ],
        out_specs=pl.BlockSpec((tm, d), lambda i: (i, 0)),
        out_shape=jax.ShapeDtypeStruct((m, d), F32),
        scratch_shapes=[pltpu.VMEM((k, d), BF16)],
        compiler_params=_params(1),
        name="out_proj_residual",
    )(a, w, g, x)


def _ffn_kernel(x_ref, gpre_ref, wg_ref, wu_ref, wd_ref, gpost_ref, o_ref, h_ref):
    f = pl.program_id(1)

    @pl.when(f == 0)
    def _():
        x = x_ref[...]
        h_ref[...] = (x * _rms_scale(x, NORM_EPS) * gpre_ref[...]).astype(BF16)
        o_ref[...] = jnp.zeros(o_ref.shape, F32)

    h = h_ref[...]
    gate = jnp.dot(h, wg_ref[...].astype(BF16), preferred_element_type=F32)
    up = jnp.dot(h, wu_ref[...].astype(BF16), preferred_element_type=F32)
    act = (gate * _sigmoid(gate) * up).astype(BF16)
    o_ref[...] += jnp.dot(act, wd_ref[...].astype(BF16), preferred_element_type=F32)

    @pl.when(f == pl.num_programs(1) - 1)
    def _():
        y = o_ref[...]
        o_ref[...] = x_ref[...] + y * _rms_scale(y, NORM_EPS) * gpost_ref[...]


def _ffn(x, gpre, wg, wu, wd, gpost, layer, *, tm, tf):
    m, d = x.shape
    dff = wg.shape[2]
    rows_once = dict(pipeline_mode=pl.Buffered(1))
    return pl.pallas_call(
        _ffn_kernel,
        grid=(m // tm, dff // tf),
        in_specs=[
            pl.BlockSpec((tm, d), lambda i, f: (i, 0), **rows_once),
            pl.BlockSpec((1, d), lambda i, f: (0, 0)),
            pl.BlockSpec((None, d, tf), lambda i, f: (layer, 0, f)),
            pl.BlockSpec((None, d, tf), lambda i, f: (layer, 0, f)),
            pl.BlockSpec((None, tf, d), lambda i, f: (layer, f, 0)),
            pl.BlockSpec((1, d), lambda i, f: (0, 0)),
        ],
        out_specs=pl.BlockSpec((tm, d), lambda i, f: (i, 0), **rows_once),
        out_shape=jax.ShapeDtypeStruct((m, d), F32),
        scratch_shapes=[pltpu.VMEM((tm, d), BF16)],
        compiler_params=_params(2),
        name="ffn",
    )(x, gpre, wg, wu, wd, gpost)


def _lru_gates(xc, wa, ba, wx, bx, sp):
    xcb = xc.astype(BF16)
    gate_a = _sigmoid(jnp.dot(xcb, wa, preferred_element_type=F32) + ba)
    gate_x = _sigmoid(jnp.dot(xcb, wx, preferred_element_type=F32) + bx)
    log_a = (-LRU_C * gate_a) * sp
    a = jnp.exp(log_a)
    y = -jnp.tanh(log_a) * (a * a + 1.0)
    b = jnp.where(y > 0.0, y * lax.rsqrt(y), 0.0) * (gate_x * xc)
    return a, b


def _lru_seq_kernel(xr_ref, yg_ref, conv0_ref, h0_ref, cw_ref, cb_ref, wa_ref, ba_ref, wx_ref, bx_ref, ap_ref,
                    o_ref, hlast_ref, xe_ref, h_ref, a_ref, b_ref, *, tt):
    ti = pl.program_id(1)
    pad = SUBLANES

    @pl.when(ti == 0)
    def _():
        xe_ref[0:pad, :] = conv0_ref[0]
        h_ref[...] = h0_ref[0]

    xe_ref[pad:pad + tt, :] = xr_ref[...]
    sp = _softplus(-ap_ref[...])
    groups = tt // SUBLANES
    rowmod = lax.broadcasted_iota(jnp.int32, (groups, SUBLANES, LRU_BLOCK), 1)

    for n in range(LRU_BLOCKS):
        cs = slice(n * LRU_BLOCK, (n + 1) * LRU_BLOCK)
        xg = xe_ref[:, cs].reshape(groups + 1, SUBLANES, LRU_BLOCK)
        xc = cb_ref[:, cs] + xg[1:].reshape(tt, LRU_BLOCK) * cw_ref[CONV_WIDTH - 1:CONV_WIDTH, cs]
        for k in range(1, CONV_WIDTH):
            rot = pltpu.roll(xg, k, axis=1)
            shifted = jnp.where(rowmod >= k, rot[1:], rot[:-1]).reshape(tt, LRU_BLOCK)
            xc = xc + shifted * cw_ref[CONV_WIDTH - 1 - k:CONV_WIDTH - k, cs]
        a, b = _lru_gates(xc, wa_ref[n], ba_ref[n], wx_ref[n], bx_ref[n], sp[:, cs])
        a = a.reshape(tt // SUBLANES, SUBLANES, LRU_BLOCK)
        b = b.reshape(tt // SUBLANES, SUBLANES, LRU_BLOCK)
        d = 1
        while d < SUBLANES:
            valid = rowmod >= d
            a_sh = jnp.where(valid, pltpu.roll(a, d, axis=1), 1.0)
            b_sh = jnp.where(valid, pltpu.roll(b, d, axis=1), 0.0)
            b = a * b_sh + b
            a = a * a_sh
            d *= 2
        a_ref[:, cs] = a.reshape(tt, LRU_BLOCK)
        b_ref[:, cs] = b.reshape(tt, LRU_BLOCK)

    def group(gi, carry):
        r = pl.multiple_of(gi * SUBLANES, SUBLANES)
        h = a_ref[pl.ds(r, SUBLANES), :] * carry + b_ref[pl.ds(r, SUBLANES), :]
        b_ref[pl.ds(r, SUBLANES), :] = h
        return h[SUBLANES - 1:SUBLANES, :]

    carry = lax.fori_loop(0, tt // SUBLANES, group, h_ref[...], unroll=4)
    h_ref[...] = carry
    hlast_ref[0] = carry
    o_ref[...] = (b_ref[...] * yg_ref[...]).astype(o_ref.dtype)
    xe_ref[0:pad, :] = xe_ref[tt:tt + pad, :]


def _lru_seq(proj, conv0, h0, cw, cb, wa, ba, wx, bx, ap, *, batch, seq, tt):
    w = LRU_WIDTH
    nt = seq // tt
    const2 = lambda b, t: (0, 0)
    const3 = lambda b, t: (0, 0, 0)
    return pl.pallas_call(
        functools.partial(_lru_seq_kernel, tt=tt),
        grid=(batch, nt),
        in_specs=[
            pl.BlockSpec((tt, w), lambda b, t: (b * nt + t, 1)),
            pl.BlockSpec((tt, w), lambda b, t: (b * nt + t, 0)),
            pl.BlockSpec((1, SUBLANES, w), lambda b, t: (b, 0, 0)),
            pl.BlockSpec((1, 1, w), lambda b, t: (b, 0, 0)),
            pl.BlockSpec((CONV_WIDTH, w), const2),
            pl.BlockSpec((1, w), const2),
            pl.BlockSpec((LRU_BLOCKS, LRU_BLOCK, LRU_BLOCK), const3),
            pl.BlockSpec((LRU_BLOCKS, 1, LRU_BLOCK), const3),
            pl.BlockSpec((LRU_BLOCKS, LRU_BLOCK, LRU_BLOCK), const3),
            pl.BlockSpec((LRU_BLOCKS, 1, LRU_BLOCK), const3),
            pl.BlockSpec((1, w), const2),
        ],
        out_specs=[
            pl.BlockSpec((tt, w), lambda b, t: (b * nt + t, 0)),
            pl.BlockSpec((1, 1, w), lambda b, t: (b, 0, 0)),
        ],
        out_shape=[
            jax.ShapeDtypeStruct((batch * seq, w), BF16),
            jax.ShapeDtypeStruct((batch, 1, w), F32),
        ],
        scratch_shapes=[
            pltpu.VMEM((tt + SUBLANES, w), F32),
            pltpu.VMEM((1, w), F32),
            pltpu.VMEM((tt, w), F32),
            pltpu.VMEM((tt, w), F32),
        ],
        compiler_params=_params(2),
        name="lru_seq",
    )(proj, proj, conv0, h0, cw, cb, wa, ba, wx, bx, ap)


def _lru_step_kernel(xr_ref, yg_ref, conv_ref, h0_ref, cw_ref, cb_ref, wa_ref, ba_ref, wx_ref, bx_ref, ap_ref,
                     o_ref, hnew_ref):
    sp = _softplus(-ap_ref[...])
    for n in range(LRU_BLOCKS):
        cs = slice(n * LRU_BLOCK, (n + 1) * LRU_BLOCK)
        xc = cb_ref[:, cs] + conv_ref[0, :, cs] * cw_ref[0:1, cs]
        for tap in range(1, CONV_WIDTH - 1):
            xc = xc + conv_ref[tap, :, cs] * cw_ref[tap:tap + 1, cs]
        xc = xc + xr_ref[:, cs] * cw_ref[CONV_WIDTH - 1:CONV_WIDTH, cs]
        a, b = _lru_gates(xc, wa_ref[n], ba_ref[n], wx_ref[n], bx_ref[n], sp[:, cs])
        h = a * h0_ref[:, cs] + b
        hnew_ref[:, cs] = h
        o_ref[:, cs] = h * yg_ref[:, cs]


def _lru_step(proj, conv_t, h0, cw, cb, wa, ba, wx, bx, ap):
    m = proj.shape[0]
    w = LRU_WIDTH
    c2 = lambda i: (0, 0)
    c3 = lambda i: (0, 0, 0)
    return pl.pallas_call(
        _lru_step_kernel,
        grid=(1,),
        in_specs=[
            pl.BlockSpec((m, w), lambda i: (0, 1)),
            pl.BlockSpec((m, w), lambda i: (0, 0)),
            pl.BlockSpec((CONV_WIDTH - 1, m, w), c3),
            pl.BlockSpec((m, w), c2),
            pl.BlockSpec((CONV_WIDTH, w), c2),
            pl.BlockSpec((1, w), c2),
            pl.BlockSpec((LRU_BLOCKS, LRU_BLOCK, LRU_BLOCK), c3),
            pl.BlockSpec((LRU_BLOCKS, 1, LRU_BLOCK), c3),
            pl.BlockSpec((LRU_BLOCKS, LRU_BLOCK, LRU_BLOCK), c3),
            pl.BlockSpec((LRU_BLOCKS, 1, LRU_BLOCK), c3),
            pl.BlockSpec((1, w), c2),
        ],
        out_specs=[pl.BlockSpec((m, w), c2), pl.BlockSpec((m, w), c2)],
        out_shape=[jax.ShapeDtypeStruct((m, w), F32), jax.ShapeDtypeStruct((m, w), F32)],
        compiler_params=_params(1),
        name="lru_step",
    )(proj, proj, conv_t, h0, cw, cb, wa, ba, wx, bx, ap)


def _lambda(lam_ref, lam_init):
    s1 = jnp.sum(lam_ref[0:1, :] * lam_ref[1:2, :], axis=-1, keepdims=True)
    s2 = jnp.sum(lam_ref[2:3, :] * lam_ref[3:4, :], axis=-1, keepdims=True)
    return jnp.exp(s1) - jnp.exp(s2) + lam_init


def _subln(o, g, lam_init):
    return o * _rms_scale(o, SUBLN_EPS) * g * (1.0 - lam_init)


def _nt_dot(a, b):
    return lax.dot_general(a, b, (((1,), (1,)), ((), ())), preferred_element_type=F32)


def _attn_prompt_kernel(lam_ref, g_ref, qb_ref, kb_ref, vb_ref, o_ref, m_ref, l_ref, acc_ref,
                        *, tile, lam_init):
    maps = range(2)
    hs = [slice(s * HEAD_DIM, (s + 1) * HEAD_DIM) for s in maps]
    seq = qb_ref.shape[0]
    half = tile // 2
    lam = _lambda(lam_ref, lam_init)

    def lane_groups(x):
        return [x[:, j * LANES:(j + 1) * LANES] for j in range(x.shape[1] // LANES)]

    def scores(q0, rows, k0, klen, diagonal):
        kc = kb_ref[pl.ds(k0, klen), :]
        sc = [_nt_dot(qb_ref[q0:q0 + rows, hs[s]], kc[:, hs[s]]) for s in maps]
        if diagonal:
            keep = (lax.broadcasted_iota(jnp.int32, (rows, klen), 1)
                    <= lax.broadcasted_iota(jnp.int32, (rows, klen), 0))
            sc = [jnp.where(keep, x, -jnp.inf) for x in sc]
        return sc

    for qi in range(seq // tile):
        base = qi * tile

        def max_block(q0, rows, k0, klen, diagonal, base=base):
            sc = scores(q0, rows, k0, klen, diagonal)
            r = slice(q0 - base, q0 - base + rows)
            m_old = [m_ref[s, r] for s in maps]
            for s in maps:
                m_ref[s, r] = functools.reduce(jnp.maximum, lane_groups(sc[s]), m_old[s])

        def sum_block(q0, rows, k0, klen, diagonal, base=base):
            sc = scores(q0, rows, k0, klen, diagonal)
            vc = vb_ref[pl.ds(k0, klen), :]
            r = slice(q0 - base, q0 - base + rows)
            m_row = [m_ref[s, r] for s in maps]
            l_old = [l_ref[s, r] for s in maps]
            acc_old = [acc_ref[s, r] for s in maps]
            p = [[jnp.exp2(x - m_row[s]) for x in lane_groups(sc[s])] for s in maps]
            for s in maps:
                l_ref[s, r] = functools.reduce(jnp.add, p[s], l_old[s])
                pb = jnp.concatenate(p[s], axis=1).astype(BF16)
                acc_ref[s, r] = acc_old[s] + jnp.dot(pb, vc, preferred_element_type=F32)

        def sweep(block, qi=qi, base=base):
            def left(c, carry):
                block(base, tile, pl.multiple_of(c * tile, tile), tile, False)
                return carry
            if qi:
                lax.fori_loop(0, qi, left, 0)
            block(base, half, base, half, True)
            block(base + half, half, base, half, False)
            block(base + half, half, base + half, half, True)

        m_ref[...] = jnp.full(m_ref.shape, -jnp.inf, F32)
        sweep(max_block)
        for s in maps:
            m_ref[s] = jnp.broadcast_to(jnp.max(m_ref[s], axis=-1, keepdims=True), (tile, LANES))
        l_ref[...] = jnp.zeros(l_ref.shape, F32)
        acc_ref[...] = jnp.zeros(acc_ref.shape, F32)
        sweep(sum_block)
        inv = [1.0 / jnp.sum(l_ref[s], axis=-1, keepdims=True) for s in maps]
        o = acc_ref[0] * inv[0] - lam * (acc_ref[1] * inv[1])
        o_ref[base:base + tile, :] = _subln(o, g_ref[...], lam_init).astype(o_ref.dtype)


def _attn_prompt(qkv, lam4, g, *, batch, seq, tile, lam_init):
    hw = 2 * HEAD_DIM
    part = D_MODEL // hw
    return pl.pallas_call(
        functools.partial(_attn_prompt_kernel, tile=tile, lam_init=lam_init),
        grid=(batch, N_HEADS),
        in_specs=[
            pl.BlockSpec((4, HEAD_DIM), lambda b, h: (0, 0)),
            pl.BlockSpec((1, hw), lambda b, h: (0, 0)),
            pl.BlockSpec((seq, hw), lambda b, h: (b, h)),
            pl.BlockSpec((seq, hw), lambda b, h: (b, part + h)),
            pl.BlockSpec((seq, hw), lambda b, h: (b, 2 * part + h)),
        ],
        out_specs=pl.BlockSpec((seq, hw), lambda b, h: (b, h)),
        out_shape=jax.ShapeDtypeStruct((batch * seq, D_MODEL), BF16),
        scratch_shapes=[
            pltpu.VMEM((2, tile, LANES), F32),
            pltpu.VMEM((2, tile, LANES), F32),
            pltpu.VMEM((2, tile, hw), F32),
        ],
        compiler_params=_params(2),
        name="attn_prompt",
    )(lam4, g, qkv, qkv, qkv)


def _attn_decode_kernel(pt_ref, lam_ref, g_ref, q_ref, knew_ref, vnew_ref, *rest, pages_per_step, page, lam_init):
    del pt_ref
    npg = pages_per_step
    k_refs = rest[:npg]
    v_refs = rest[npg:2 * npg]
    o_ref, m_ref, l_ref, acc_ref = rest[2 * npg:]
    step = pl.program_id(1)
    n_maps = 2 * N_HEADS
    cols = page * N_HEADS

    @pl.when(step == 0)
    def _():
        m_ref[...] = jnp.full(m_ref.shape, -jnp.inf, F32)
        l_ref[...] = jnp.zeros(l_ref.shape, F32)
        acc_ref[...] = jnp.zeros(acc_ref.shape, F32)

    q = q_ref[0]
    qb = q.astype(BF16)
    row = lax.broadcasted_iota(jnp.int32, (n_maps, cols), 0)
    col = lax.broadcasted_iota(jnp.int32, (n_maps, cols), 1)
    first_map = row < N_HEADS
    own = (col & (N_HEADS - 1)) == (row & (N_HEADS - 1))

    scores = []
    for j in range(npg):
        k_even = k_refs[j][:, pl.ds(0, N_HEADS, stride=2), :].reshape(cols, HEAD_DIM).astype(BF16)
        k_odd = k_refs[j][:, pl.ds(1, N_HEADS, stride=2), :].reshape(cols, HEAD_DIM).astype(BF16)
        sc = jnp.where(first_map, _nt_dot(qb, k_even), _nt_dot(qb, k_odd))
        scores.append(jnp.where(own, sc, -jnp.inf))
    sc = jnp.concatenate(scores, axis=1) if npg > 1 else scores[0]
    m_old = m_ref[...]
    m_new = jnp.maximum(m_old, jnp.max(sc, axis=-1, keepdims=True))
    alpha = jnp.exp2(m_old - m_new)
    p = jnp.exp2(sc - m_new)
    l_ref[...] = alpha * l_ref[...] + jnp.sum(p, axis=-1, keepdims=True)
    pb = p.astype(BF16)
    pv = None
    for j in range(npg):
        vj = v_refs[j][...].reshape(cols, 2 * HEAD_DIM).astype(BF16)
        part = jnp.dot(pb[:, j * cols:(j + 1) * cols], vj, preferred_element_type=F32)
        pv = part if pv is None else pv + part
    acc_ref[...] = alpha * acc_ref[...] + pv
    m_ref[...] = m_new

    @pl.when(step == pl.num_programs(1) - 1)
    def _():
        s_new = jnp.sum(q * knew_ref[0], axis=-1, keepdims=True)
        m_old2 = m_ref[...]
        m_fin = jnp.maximum(m_old2, s_new)
        alpha2 = jnp.exp2(m_old2 - m_fin)
        p_new = jnp.exp2(s_new - m_fin)
        l_fin = alpha2 * l_ref[...] + p_new
        v_new = vnew_ref[0]
        acc = (alpha2 * acc_ref[...] + p_new * jnp.concatenate([v_new, v_new], axis=0)) / l_fin
        lam = _lambda(lam_ref, lam_init)
        o = acc[0:N_HEADS, :] - lam * acc[N_HEADS:n_maps, :]
        o_ref[0] = _subln(o, g_ref[...], lam_init)


def _attn_decode(q, k_new, v_new, cache_k, cache_v, layer, page_table, lam4, g, *, pages_per_step, lam_init):
    rows = q.shape[0]
    n_pages = page_table.shape[1]
    page = cache_k.shape[2]
    npg = pages_per_step
    hw = 2 * HEAD_DIM
    n_maps = 2 * N_HEADS

    def page_spec(j, heads, width):
        return pl.BlockSpec((None, None, page, heads, width),
                            lambda b, s, pt: (layer, pt[b, s * npg + j], 0, 0, 0))

    qk_spec = pl.BlockSpec((1, n_maps, HEAD_DIM), lambda b, s, pt: (b, 0, 0))
    v_spec = pl.BlockSpec((1, N_HEADS, hw), lambda b, s, pt: (b, 0, 0))
    grid_spec = pltpu.PrefetchScalarGridSpec(
        num_scalar_prefetch=1,
        grid=(rows, n_pages // npg),
        in_specs=[
            pl.BlockSpec((4, HEAD_DIM), lambda b, s, pt: (0, 0)),
            pl.BlockSpec((1, hw), lambda b, s, pt: (0, 0)),
            qk_spec, qk_spec, v_spec,
        ] + [page_spec(j, n_maps, HEAD_DIM) for j in range(npg)] + [page_spec(j, N_HEADS, hw) for j in range(npg)],
        out_specs=v_spec,
        scratch_shapes=[
            pltpu.VMEM((n_maps, 1), F32),
            pltpu.VMEM((n_maps, 1), F32),
            pltpu.VMEM((n_maps, hw), F32),
        ],
    )
    return pl.pallas_call(
        functools.partial(_attn_decode_kernel, pages_per_step=npg, page=page, lam_init=lam_init),
        grid_spec=grid_spec,
        out_shape=jax.ShapeDtypeStruct((rows, N_HEADS, hw), F32),
        compiler_params=_params(2),
        name="attn_decode",
    )(page_table, lam4, g, q, k_new, v_new, *([cache_k] * npg), *([cache_v] * npg))


def _rope_tables(pos):
    half = HEAD_DIM // 2
    inv_freq = jnp.power(ROPE_THETA, -jnp.arange(half, dtype=F32) * (2.0 / HEAD_DIM))
    ang = pos.astype(F32)[:, None] * inv_freq[None, :]
    cos = jnp.cos(ang)
    sin = jnp.sin(ang)
    return jnp.concatenate([cos, cos], axis=-1), jnp.concatenate([-sin, sin], axis=-1)


def kernel(x_prompt, x_sample, state_conv, state_h, cache_k, cache_v, page_table, norm_mix_pre, norm_mix_post, norm_ffn_pre, norm_ffn_post, lru_w_in, lru_b_in, lru_conv_w, lru_conv_b, lru_w_gate_a, lru_b_gate_a, lru_w_gate_x, lru_b_gate_x, lru_a_param, lru_w_out, attn_w_qkv, attn_lambda_q1, attn_lambda_k1, attn_lambda_q2, attn_lambda_k2, attn_subln, attn_w_o, ffn_w_gate, ffn_w_up, ffn_w_down):
    bp, seq, d = x_prompt.shape
    bd, t_dec, _ = x_sample.shape
    assert d == D_MODEL and t_dec == 1
    depth = norm_mix_pre.shape[0]
    n_pages = page_table.shape[1]
    page = cache_k.shape[2]
    past = n_pages * page
    mp = bp * seq

    xp = x_prompt.reshape(mp, d)
    xs = x_sample.reshape(bd, d)
    row = lambda v: v.reshape(1, -1)

    conv_p, h_p, k_p, v_p = [], [], [], []
    conv_s, h_s, k_s, v_s = [], [], [], []
    for i in range(depth):
        j = i // N_MIXERS
        g_pre = row(norm_mix_pre[i])
        g_post = row(norm_mix_post[i])
        if i % N_MIXERS == 0:
            w_in = lru_w_in[j].astype(BF16)
            b_in = row(lru_b_in[j])
            w_out = lru_w_out[j]
            wa = lru_w_gate_a[j].astype(BF16)
            wx = lru_w_gate_x[j].astype(BF16)
            ba = lru_b_gate_a[j].reshape(LRU_BLOCKS, 1, LRU_BLOCK)
            bx = lru_b_gate_x[j].reshape(LRU_BLOCKS, 1, LRU_BLOCK)
            gate_prm = (lru_conv_w[j], row(lru_conv_b[j]), wa, ba, wx, bx, row(lru_a_param[j]))
            proj_p = _lru_in_proj(xp, g_pre, w_in, b_in, tm=1024, tn=1024)
            hy_p, hl_p = _lru_seq(proj_p, jnp.zeros((bp, SUBLANES, LRU_WIDTH), F32), jnp.zeros((bp, 1, LRU_WIDTH), F32),
                                  *gate_prm, batch=bp, seq=seq, tt=256)
            conv_p.append(proj_p.reshape(bp, seq, 2 * LRU_WIDTH)[:, seq - (CONV_WIDTH - 1):, LRU_WIDTH:])
            h_p.append(hl_p.reshape(bp, LRU_WIDTH))
            xp = _out_proj_residual(hy_p, w_out, g_post, xp, tm=512)
            proj_s = _lru_in_proj(xs, g_pre, w_in, b_in, tm=bd, tn=1024)
            hy_s, hl_s = _lru_step(proj_s, state_conv[j].transpose(1, 0, 2), state_h[j], *gate_prm)
            conv_s.append(jnp.concatenate([state_conv[j][:, 1:], proj_s[:, None, LRU_WIDTH:]], axis=1))
            h_s.append(hl_s)
            xs = _out_proj_residual(hy_s, w_out, g_post, xs, tm=bd)
        else:
            lam_init = 0.8 - 0.6 * math.exp(-0.3 * i)
            w_qkv = attn_w_qkv[j].astype(BF16)
            w_o = attn_w_o[j]
            lam4 = jnp.stack([attn_lambda_q1[j], attn_lambda_k1[j], attn_lambda_q2[j], attn_lambda_k2[j]])
            g_sub = row(attn_subln[j])
            cos_p, sin_p = _rope_tables(jnp.arange(seq))
            qkv_p, k4_p, v4_p = _qkv_proj(xp, g_pre, w_qkv, cos_p, sin_p, tm=1024, tn=1024)
            o = _attn_prompt(qkv_p, lam4, g_sub, batch=bp, seq=seq, tile=512, lam_init=lam_init)
            k_p.append(k4_p.reshape(bp, seq, 2 * N_HEADS, HEAD_DIM))
            v_p.append(v4_p.reshape(bp, seq, N_HEADS, 2 * HEAD_DIM))
            xp = _out_proj_residual(o, w_o, g_post, xp, tm=512)
            cos_s, sin_s = _rope_tables(jnp.full((bd,), past))
            qkv_s, ks_new, vs_new = _qkv_proj(xs, g_pre, w_qkv, cos_s, sin_s, tm=bd, tn=1024)
            map_major = lambda a: a.reshape(bd, N_HEADS, 2, HEAD_DIM).transpose(0, 2, 1, 3).reshape(bd, 2 * N_HEADS, HEAD_DIM)
            os_ = _attn_decode(map_major(qkv_s[:, :d].astype(F32)), map_major(ks_new), vs_new,
                               cache_k, cache_v, j, page_table, lam4, g_sub, pages_per_step=8, lam_init=lam_init)
            k_s.append(ks_new.reshape(bd, 1, 2 * N_HEADS, HEAD_DIM))
            v_s.append(vs_new.reshape(bd, 1, N_HEADS, 2 * HEAD_DIM))
            xs = _out_proj_residual(os_.reshape(bd, d), w_o, g_post, xs, tm=bd)
        gf_pre = row(norm_ffn_pre[i])
        gf_post = row(norm_ffn_post[i])
        xp = _ffn(xp, gf_pre, ffn_w_gate, ffn_w_up, ffn_w_down, gf_post, i, tm=1024, tf=512)
        xs = _ffn(xs, gf_pre, ffn_w_gate, ffn_w_up, ffn_w_down, gf_post, i, tm=bd, tf=512)
    return (xp.reshape(bp, seq, d), xs.reshape(bd, 1, d), jnp.stack(conv_p), jnp.stack(h_p), jnp.stack(k_p), jnp.stack(v_p),
            jnp.stack(conv_s), jnp.stack(h_s), jnp.stack(k_s), jnp.stack(v_s))
```

```python
import functools
import math

import jax
import jax.numpy as jnp
from jax import lax
from jax.experimental import pallas as pl
from jax.experimental.pallas import tpu as pltpu

D_MODEL = 2048
LRU_WIDTH = D_MODEL
LRU_BLOCK = 256
LRU_BLOCKS = LRU_WIDTH // LRU_BLOCK
CONV_WIDTH = 4
LRU_C = 8.0
HEAD_DIM = 128
N_HEADS = D_MODEL // (2 * HEAD_DIM)
ROPE_THETA = 10000.0
NORM_EPS = 1e-6
SUBLN_EPS = 1e-5
N_MIXERS = 2
Q_SCALE = HEAD_DIM ** -0.5 * math.log2(math.e)

SUBLANES = 8
LANES = 128
VMEM_CAPACITY_BYTES = 64 * 1024 * 1024
VMEM_LIMIT_BYTES = VMEM_CAPACITY_BYTES - 4 * 1024 * 1024

BF16 = jnp.bfloat16
F32 = jnp.float32


def _params(n_grid_dims):
    return pltpu.CompilerParams(
        dimension_semantics=("arbitrary",) * n_grid_dims,
        vmem_limit_bytes=VMEM_LIMIT_BYTES,
    )


def _rms_scale(x, eps):
    return lax.rsqrt(jnp.mean(x * x, axis=-1, keepdims=True) + eps)


def _sigmoid(x):
    return 1.0 / (1.0 + jnp.exp(-x))


def _gelu_tanh(x):
    c = math.sqrt(2.0 / math.pi)
    return 0.5 * x * (1.0 + jnp.tanh(c * (x + 0.044715 * (x * x * x))))


def _softplus(z):
    return jnp.maximum(z, 0.0) + jnp.log1p(jnp.exp(-jnp.abs(z)))


def _norm_once(x_ref, g_ref, h_ref):
    @pl.when(pl.program_id(1) == 0)
    def _():
        x = x_ref[...]
        h_ref[...] = (x * _rms_scale(x, NORM_EPS) * g_ref[...]).astype(BF16)


def _lru_in_kernel(x_ref, g_ref, w_ref, b_ref, o_ref, h_ref, *, n_gate_tiles):
    _norm_once(x_ref, g_ref, h_ref)
    j = pl.program_id(1)
    proj = lambda: jnp.dot(h_ref[...], w_ref[...], preferred_element_type=F32) + b_ref[...]

    @pl.when(j < n_gate_tiles)
    def _():
        o_ref[...] = _gelu_tanh(proj())

    @pl.when(j >= n_gate_tiles)
    def _():
        o_ref[...] = proj()


def _lru_in_proj(x, g, w, b, *, tm, tn):
    m, d = x.shape
    n = w.shape[1]
    return pl.pallas_call(
        functools.partial(_lru_in_kernel, n_gate_tiles=LRU_WIDTH // tn),
        grid=(m // tm, n // tn),
        in_specs=[
            pl.BlockSpec((tm, d), lambda i, j: (i, 0)),
            pl.BlockSpec((1, d), lambda i, j: (0, 0)),
            pl.BlockSpec((d, tn), lambda i, j: (0, j)),
            pl.BlockSpec((1, tn), lambda i, j: (0, j)),
        ],
        out_specs=pl.BlockSpec((tm, tn), lambda i, j: (i, j)),
        out_shape=jax.ShapeDtypeStruct((m, n), F32),
        scratch_shapes=[pltpu.VMEM((tm, d), BF16)],
        compiler_params=_params(2),
        name="lru_in_proj",
    )(x, g, w, b)


def _rope(x, cos2, sin2):
    return x * cos2 + pltpu.roll(x, HEAD_DIM // 2, axis=1) * sin2


def _qkv_kernel(x_ref, g_ref, w_ref, cos_ref, sin_ref, o_ref, k4_ref, v4_ref, h_ref, *, tiles_per_part, heads_per_tile):
    _norm_once(x_ref, g_ref, h_ref)
    j = pl.program_id(1)
    tm = o_ref.shape[0]
    proj = lambda: jnp.dot(h_ref[...], w_ref[...], preferred_element_type=F32)

    def roped(acc, scale):
        cos2 = cos_ref[...]
        sin2 = sin_ref[...]
        parts = []
        for hd in range(heads_per_tile):
            r = _rope(acc[:, hd * HEAD_DIM:(hd + 1) * HEAD_DIM], cos2, sin2)
            parts.append(r if scale is None else r * scale)
        return jnp.concatenate(parts, axis=1)

    @pl.when(j < tiles_per_part)
    def _():
        o_ref[...] = roped(proj(), Q_SCALE)

    @pl.when((j >= tiles_per_part) & (j < 2 * tiles_per_part))
    def _():
        k = roped(proj(), None)
        o_ref[...] = k
        k4_ref[...] = k.reshape(tm, heads_per_tile, HEAD_DIM)

    v_heads = heads_per_tile // 2
    for t in range(tiles_per_part):
        @pl.when(j == 2 * tiles_per_part + t)
        def _(t=t):
            v = proj()
            o_ref[...] = v
            v4_ref[:, t * v_heads:(t + 1) * v_heads, :] = v.reshape(tm, v_heads, 2 * HEAD_DIM)


def _qkv_proj(x, g, w, cos2, sin2, *, tm, tn):
    m, d = x.shape
    n = w.shape[1]
    tpp = d // tn
    hpt = tn // HEAD_DIM
    n_pos_tiles = cos2.shape[0] // tm
    return pl.pallas_call(
        functools.partial(_qkv_kernel, tiles_per_part=tpp, heads_per_tile=hpt),
        grid=(m // tm, n // tn),
        in_specs=[
            pl.BlockSpec((tm, d), lambda i, j: (i, 0), pipeline_mode=pl.Buffered(1)),
            pl.BlockSpec((1, d), lambda i, j: (0, 0)),
            pl.BlockSpec((d, tn), lambda i, j: (0, j)),
            pl.BlockSpec((tm, HEAD_DIM), lambda i, j: (i % n_pos_tiles, 0)),
            pl.BlockSpec((tm, HEAD_DIM), lambda i, j: (i % n_pos_tiles, 0)),
        ],
        out_specs=[
            pl.BlockSpec((tm, tn), lambda i, j: (i, j)),
            pl.BlockSpec((tm, hpt, HEAD_DIM), lambda i, j: (i, jnp.clip(j - tpp, 0, tpp - 1), 0)),
            pl.BlockSpec((tm, N_HEADS, 2 * HEAD_DIM), lambda i, j: (i, 0, 0)),
        ],
        out_shape=[
            jax.ShapeDtypeStruct((m, n), F32),
            jax.ShapeDtypeStruct((m, 2 * N_HEADS, HEAD_DIM), F32),
            jax.ShapeDtypeStruct((m, N_HEADS, 2 * HEAD_DIM), F32),
        ],
        scratch_shapes=[pltpu.VMEM((tm, d), BF16)],
        compiler_params=_params(2),
        name="qkv_proj",
    )(x, g, w, cos2, sin2)


def _out_proj_kernel(a_ref, w_ref, g_ref, x_ref, o_ref):
    y = jnp.dot(a_ref[...].astype(BF16), w_ref[...], preferred_element_type=F32)
    o_ref[...] = x_ref[...] + y * _rms_scale(y, NORM_EPS) * g_ref[...]


def _out_proj_residual(a, w, g, x, *, tm):
    m, k = a.shape
    d = w.shape[1]
    return pl.pallas_call(
        _out_proj_kernel,
        grid=(m // tm,),
        in_specs=[
            pl.BlockSpec((tm, k), lambda i: (i, 0)),
            pl.BlockSpec((k, d), lambda i: (0, 0)),
            pl.BlockSpec((1, d), lambda i: (0, 0)),
            pl.BlockSpec((tm, d), lambda i: (i, 0)),
        ],
        out_specs=pl.BlockSpec((tm, d), lambda i: (i, 0)),
        out_shape=jax.ShapeDtypeStruct((m, d), F32),
        compiler_params=_params(1),
        name="out_proj_residual",
    )(a, w, g, x)


def _ffn_kernel(x_ref, gpre_ref, wg_ref, wu_ref, wd_ref, gpost_ref, o_ref, h_ref):
    f = pl.program_id(1)

    @pl.when(f == 0)
    def _():
        x = x_ref[...]
        h_ref[...] = (x * _rms_scale(x, NORM_EPS) * gpre_ref[...]).astype(BF16)
        o_ref[...] = jnp.zeros(o_ref.shape, F32)

    h = h_ref[...]
    gate = jnp.dot(h, wg_ref[...].astype(BF16), preferred_element_type=F32)
    up = jnp.dot(h, wu_ref[...].astype(BF16), preferred_element_type=F32)
    act = (gate * _sigmoid(gate) * up).astype(BF16)
    o_ref[...] += jnp.dot(act, wd_ref[...].astype(BF16), preferred_element_type=F32)

    @pl.when(f == pl.num_programs(1) - 1)
    def _():
        y = o_ref[...]
        o_ref[...] = x_ref[...] + y * _rms_scale(y, NORM_EPS) * gpost_ref[...]


def _ffn(x, gpre, wg, wu, wd, gpost, layer, *, tm, tf):
    m, d = x.shape
    dff = wg.shape[2]
    rows_once = dict(pipeline_mode=pl.Buffered(1))
    return pl.pallas_call(
        _ffn_kernel,
        grid=(m // tm, dff // tf),
        in_specs=[
            pl.BlockSpec((tm, d), lambda i, f: (i, 0), **rows_once),
            pl.BlockSpec((1, d), lambda i, f: (0, 0)),
            pl.BlockSpec((None, d, tf), lambda i, f: (layer, 0, f)),
            pl.BlockSpec((None, d, tf), lambda i, f: (layer, 0, f)),
            pl.BlockSpec((None, tf, d), lambda i, f: (layer, f, 0)),
            pl.BlockSpec((1, d), lambda i, f: (0, 0)),
        ],
        out_specs=pl.BlockSpec((tm, d), lambda i, f: (i, 0), **rows_once),
        out_shape=jax.ShapeDtypeStruct((m, d), F32),
        scratch_shapes=[pltpu.VMEM((tm, d), BF16)],
        compiler_params=_params(2),
        name="ffn",
    )(x, gpre, wg, wu, wd, gpost)


def _lru_gates(xc, wa, ba, wx, bx, sp):
    xcb = xc.astype(BF16)
    gate_a = _sigmoid(jnp.dot(xcb, wa, preferred_element_type=F32) + ba)
    gate_x = _sigmoid(jnp.dot(xcb, wx, preferred_element_type=F32) + bx)
    log_a = (-LRU_C * gate_a) * sp
    a = jnp.exp(log_a)
    y = -jnp.tanh(log_a) * (a * a + 1.0)
    b = jnp.where(y > 0.0, y * lax.rsqrt(y), 0.0) * (gate_x * xc)
    return a, b


def _lru_seq_kernel(xr_ref, yg_ref, conv0_ref, h0_ref, cw_ref, cb_ref, wa_ref, ba_ref, wx_ref, bx_ref, ap_ref,
                    o_ref, hlast_ref, xe_ref, h_ref, a_ref, b_ref, *, tt):
    ti = pl.program_id(1)
    pad = SUBLANES

    @pl.when(ti == 0)
    def _():
        xe_ref[0:pad, :] = conv0_ref[0]
        h_ref[...] = h0_ref[0]

    xe_ref[pad:pad + tt, :] = xr_ref[...]
    sp = _softplus(-ap_ref[...])
    groups = tt // SUBLANES
    rowmod = lax.broadcasted_iota(jnp.int32, (groups, SUBLANES, LRU_BLOCK), 1)

    for n in range(LRU_BLOCKS):
        cs = slice(n * LRU_BLOCK, (n + 1) * LRU_BLOCK)
        xg = xe_ref[:, cs].reshape(groups + 1, SUBLANES, LRU_BLOCK)
        xc = cb_ref[:, cs] + xg[1:].reshape(tt, LRU_BLOCK) * cw_ref[CONV_WIDTH - 1:CONV_WIDTH, cs]
        for k in range(1, CONV_WIDTH):
            rot = pltpu.roll(xg, k, axis=1)
            shifted = jnp.where(rowmod >= k, rot[1:], rot[:-1]).reshape(tt, LRU_BLOCK)
            xc = xc + shifted * cw_ref[CONV_WIDTH - 1 - k:CONV_WIDTH - k, cs]
        a, b = _lru_gates(xc, wa_ref[n], ba_ref[n], wx_ref[n], bx_ref[n], sp[:, cs])
        a = a.reshape(tt // SUBLANES, SUBLANES, LRU_BLOCK)
        b = b.reshape(tt // SUBLANES, SUBLANES, LRU_BLOCK)
        d = 1
        while d < SUBLANES:
            valid = rowmod >= d
            a_sh = jnp.where(valid, pltpu.roll(a, d, axis=1), 1.0)
            b_sh = jnp.where(valid, pltpu.roll(b, d, axis=1), 0.0)
            b = a * b_sh + b
            a = a * a_sh
            d *= 2
        a_ref[:, cs] = a.reshape(tt, LRU_BLOCK)
        b_ref[:, cs] = b.reshape(tt, LRU_BLOCK)

    def group(gi, carry):
        r = pl.multiple_of(gi * SUBLANES, SUBLANES)
        h = a_ref[pl.ds(r, SUBLANES), :] * carry + b_ref[pl.ds(r, SUBLANES), :]
        b_ref[pl.ds(r, SUBLANES), :] = h
        return h[SUBLANES - 1:SUBLANES, :]

    carry = lax.fori_loop(0, tt // SUBLANES, group, h_ref[...], unroll=4)
    h_ref[...] = carry
    hlast_ref[0] = carry
    o_ref[...] = (b_ref[...] * yg_ref[...]).astype(o_ref.dtype)
    xe_ref[0:pad, :] = xe_ref[tt:tt + pad, :]


def _lru_seq(proj, conv0, h0, cw, cb, wa, ba, wx, bx, ap, *, batch, seq, tt):
    w = LRU_WIDTH
    nt = seq // tt
    const2 = lambda b, t: (0, 0)
    const3 = lambda b, t: (0, 0, 0)
    return pl.pallas_call(
        functools.partial(_lru_seq_kernel, tt=tt),
        grid=(batch, nt),
        in_specs=[
            pl.BlockSpec((tt, w), lambda b, t: (b * nt + t, 1)),
            pl.BlockSpec((tt, w), lambda b, t: (b * nt + t, 0)),
            pl.BlockSpec((1, SUBLANES, w), lambda b, t: (b, 0, 0)),
            pl.BlockSpec((1, 1, w), lambda b, t: (b, 0, 0)),
            pl.BlockSpec((CONV_WIDTH, w), const2),
            pl.BlockSpec((1, w), const2),
            pl.BlockSpec((LRU_BLOCKS, LRU_BLOCK, LRU_BLOCK), const3),
            pl.BlockSpec((LRU_BLOCKS, 1, LRU_BLOCK), const3),
            pl.BlockSpec((LRU_BLOCKS, LRU_BLOCK, LRU_BLOCK), const3),
            pl.BlockSpec((LRU_BLOCKS, 1, LRU_BLOCK), const3),
            pl.BlockSpec((1, w), const2),
        ],
        out_specs=[
            pl.BlockSpec((tt, w), lambda b, t: (b * nt + t, 0)),
            pl.BlockSpec((1, 1, w), lambda b, t: (b, 0, 0)),
        ],
        out_shape=[
            jax.ShapeDtypeStruct((batch * seq, w), BF16),
            jax.ShapeDtypeStruct((batch, 1, w), F32),
        ],
        scratch_shapes=[
            pltpu.VMEM((tt + SUBLANES, w), F32),
            pltpu.VMEM((1, w), F32),
            pltpu.VMEM((tt, w), F32),
            pltpu.VMEM((tt, w), F32),
        ],
        compiler_params=_params(2),
        name="lru_seq",
    )(proj, proj, conv0, h0, cw, cb, wa, ba, wx, bx, ap)


def _lru_step_kernel(xr_ref, yg_ref, conv_ref, h0_ref, cw_ref, cb_ref, wa_ref, ba_ref, wx_ref, bx_ref, ap_ref,
                     o_ref, hnew_ref):
    sp = _softplus(-ap_ref[...])
    for n in range(LRU_BLOCKS):
        cs = slice(n * LRU_BLOCK, (n + 1) * LRU_BLOCK)
        xc = cb_ref[:, cs] + conv_ref[0, :, cs] * cw_ref[0:1, cs]
        for tap in range(1, CONV_WIDTH - 1):
            xc = xc + conv_ref[tap, :, cs] * cw_ref[tap:tap + 1, cs]
        xc = xc + xr_ref[:, cs] * cw_ref[CONV_WIDTH - 1:CONV_WIDTH, cs]
        a, b = _lru_gates(xc, wa_ref[n], ba_ref[n], wx_ref[n], bx_ref[n], sp[:, cs])
        h = a * h0_ref[:, cs] + b
        hnew_ref[:, cs] = h
        o_ref[:, cs] = h * yg_ref[:, cs]


def _lru_step(proj, conv_t, h0, cw, cb, wa, ba, wx, bx, ap):
    m = proj.shape[0]
    w = LRU_WIDTH
    c2 = lambda i: (0, 0)
    c3 = lambda i: (0, 0, 0)
    return pl.pallas_call(
        _lru_step_kernel,
        grid=(1,),
        in_specs=[
            pl.BlockSpec((m, w), lambda i: (0, 1)),
            pl.BlockSpec((m, w), lambda i: (0, 0)),
            pl.BlockSpec((CONV_WIDTH - 1, m, w), c3),
            pl.BlockSpec((m, w), c2),
            pl.BlockSpec((CONV_WIDTH, w), c2),
            pl.BlockSpec((1, w), c2),
            pl.BlockSpec((LRU_BLOCKS, LRU_BLOCK, LRU_BLOCK), c3),
            pl.BlockSpec((LRU_BLOCKS, 1, LRU_BLOCK), c3),
            pl.BlockSpec((LRU_BLOCKS, LRU_BLOCK, LRU_BLOCK), c3),
            pl.BlockSpec((LRU_BLOCKS, 1, LRU_BLOCK), c3),
            pl.BlockSpec((1, w), c2),
        ],
        out_specs=[pl.BlockSpec((m, w), c2), pl.BlockSpec((m, w), c2)],
        out_shape=[jax.ShapeDtypeStruct((m, w), F32), jax.ShapeDtypeStruct((m, w), F32)],
        compiler_params=_params(1),
        name="lru_step",
    )(proj, proj, conv_t, h0, cw, cb, wa, ba, wx, bx, ap)


def _lambda(lam_ref, lam_init):
    s1 = jnp.sum(lam_ref[0:1, :] * lam_ref[1:2, :], axis=-1, keepdims=True)
    s2 = jnp.sum(lam_ref[2:3, :] * lam_ref[3:4, :], axis=-1, keepdims=True)
    return jnp.exp(s1) - jnp.exp(s2) + lam_init


def _subln(o, g, lam_init):
    return o * _rms_scale(o, SUBLN_EPS) * g * (1.0 - lam_init)


def _nt_dot(a, b):
    return lax.dot_general(a, b, (((1,), (1,)), ((), ())), preferred_element_type=F32)


def _attn_prompt_kernel(lam_ref, g_ref, q_ref, k_ref, v_ref, o_ref, qb_ref, kb_ref, vb_ref, m_ref, l_ref, acc_ref,
                        *, tile, lam_init):
    maps = range(2)
    hs = [slice(s * HEAD_DIM, (s + 1) * HEAD_DIM) for s in maps]
    seq = q_ref.shape[0]
    half = tile // 2
    qb_ref[...] = q_ref[...].astype(BF16)
    kb_ref[...] = k_ref[...].astype(BF16)
    vb_ref[...] = v_ref[...].astype(BF16)
    lam = _lambda(lam_ref, lam_init)

    def lane_groups(x):
        return [x[:, j * LANES:(j + 1) * LANES] for j in range(x.shape[1] // LANES)]

    def scores(q0, rows, k0, klen, diagonal):
        kc = kb_ref[pl.ds(k0, klen), :]
        sc = [_nt_dot(qb_ref[q0:q0 + rows, hs[s]], kc[:, hs[s]]) for s in maps]
        if diagonal:
            keep = (lax.broadcasted_iota(jnp.int32, (rows, klen), 1)
                    <= lax.broadcasted_iota(jnp.int32, (rows, klen), 0))
            sc = [jnp.where(keep, x, -jnp.inf) for x in sc]
        return sc

    for qi in range(seq // tile):
        base = qi * tile

        def max_block(q0, rows, k0, klen, diagonal, base=base):
            sc = scores(q0, rows, k0, klen, diagonal)
            r = slice(q0 - base, q0 - base + rows)
            m_old = [m_ref[s, r] for s in maps]
            for s in maps:
                m_ref[s, r] = functools.reduce(jnp.maximum, lane_groups(sc[s]), m_old[s])

        def sum_block(q0, rows, k0, klen, diagonal, base=base):
            sc = scores(q0, rows, k0, klen, diagonal)
            vc = vb_ref[pl.ds(k0, klen), :]
            r = slice(q0 - base, q0 - base + rows)
            m_row = [m_ref[s, r] for s in maps]
            l_old = [l_ref[s, r] for s in maps]
            acc_old = [acc_ref[s, r] for s in maps]
            p = [[jnp.exp2(x - m_row[s]) for x in lane_groups(sc[s])] for s in maps]
            for s in maps:
                l_ref[s, r] = functools.reduce(jnp.add, p[s], l_old[s])
                pb = jnp.concatenate(p[s], axis=1).astype(BF16)
                acc_ref[s, r] = acc_old[s] + jnp.dot(pb, vc, preferred_element_type=F32)

        def sweep(block, qi=qi, base=base):
            def left(c, carry):
                block(base, tile, pl.multiple_of(c * tile, tile), tile, False)
                return carry
            if qi:
                lax.fori_loop(0, qi, left, 0)
            block(base, half, base, half, True)
            block(base + half, half, base, half, False)
            block(base + half, half, base + half, half, True)

        m_ref[...] = jnp.full(m_ref.shape, -jnp.inf, F32)
        sweep(max_block)
        for s in maps:
            m_ref[s] = jnp.broadcast_to(jnp.max(m_ref[s], axis=-1, keepdims=True), (tile, LANES))
        l_ref[...] = jnp.zeros(l_ref.shape, F32)
        acc_ref[...] = jnp.zeros(acc_ref.shape, F32)
        sweep(sum_block)
        inv = [1.0 / jnp.sum(l_ref[s], axis=-1, keepdims=True) for s in maps]
        o = acc_ref[0] * inv[0] - lam * (acc_ref[1] * inv[1])
        o_ref[base:base + tile, :] = _subln(o, g_ref[...], lam_init).astype(o_ref.dtype)


def _attn_prompt(qkv, lam4, g, *, batch, seq, tile, lam_init):
    hw = 2 * HEAD_DIM
    part = D_MODEL // hw
    return pl.pallas_call(
        functools.partial(_attn_prompt_kernel, tile=tile, lam_init=lam_init),
        grid=(batch, N_HEADS),
        in_specs=[
            pl.BlockSpec((4, HEAD_DIM), lambda b, h: (0, 0)),
            pl.BlockSpec((1, hw), lambda b, h: (0, 0)),
            pl.BlockSpec((seq, hw), lambda b, h: (b, h)),
            pl.BlockSpec((seq, hw), lambda b, h: (b, part + h)),
            pl.BlockSpec((seq, hw), lambda b, h: (b, 2 * part + h)),
        ],
        out_specs=pl.BlockSpec((seq, hw), lambda b, h: (b, h)),
        out_shape=jax.ShapeDtypeStruct((batch * seq, D_MODEL), BF16),
        scratch_shapes=[
            pltpu.VMEM((seq, hw), BF16),
            pltpu.VMEM((seq, hw), BF16),
            pltpu.VMEM((seq, hw), BF16),
            pltpu.VMEM((2, tile, LANES), F32),
            pltpu.VMEM((2, tile, LANES), F32),
            pltpu.VMEM((2, tile, hw), F32),
        ],
        compiler_params=_params(2),
        name="attn_prompt",
    )(lam4, g, qkv, qkv, qkv)


def _attn_decode_kernel(pt_ref, lam_ref, g_ref, q_ref, knew_ref, vnew_ref, *rest, pages_per_step, page, lam_init):
    del pt_ref
    npg = pages_per_step
    k_refs = rest[:npg]
    v_refs = rest[npg:2 * npg]
    o_ref, m_ref, l_ref, acc_ref = rest[2 * npg:]
    step = pl.program_id(1)
    n_maps = 2 * N_HEADS
    cols = page * N_HEADS

    @pl.when(step == 0)
    def _():
        m_ref[...] = jnp.full(m_ref.shape, -jnp.inf, F32)
        l_ref[...] = jnp.zeros(l_ref.shape, F32)
        acc_ref[...] = jnp.zeros(acc_ref.shape, F32)

    q = q_ref[0]
    qb = q.astype(BF16)
    row = lax.broadcasted_iota(jnp.int32, (n_maps, cols), 0)
    col = lax.broadcasted_iota(jnp.int32, (n_maps, cols), 1)
    first_map = row < N_HEADS
    own = (col & (N_HEADS - 1)) == (row & (N_HEADS - 1))

    scores = []
    for j in range(npg):
        k_even = k_refs[j][:, pl.ds(0, N_HEADS, stride=2), :].reshape(cols, HEAD_DIM).astype(BF16)
        k_odd = k_refs[j][:, pl.ds(1, N_HEADS, stride=2), :].reshape(cols, HEAD_DIM).astype(BF16)
        sc = jnp.where(first_map, _nt_dot(qb, k_even), _nt_dot(qb, k_odd))
        scores.append(jnp.where(own, sc, -jnp.inf))
    sc = jnp.concatenate(scores, axis=1) if npg > 1 else scores[0]
    m_old = m_ref[...]
    m_new = jnp.maximum(m_old, jnp.max(sc, axis=-1, keepdims=True))
    alpha = jnp.exp2(m_old - m_new)
    p = jnp.exp2(sc - m_new)
    l_ref[...] = alpha * l_ref[...] + jnp.sum(p, axis=-1, keepdims=True)
    pb = p.astype(BF16)
    pv = None
    for j in range(npg):
        vj = v_refs[j][...].reshape(cols, 2 * HEAD_DIM).astype(BF16)
        part = jnp.dot(pb[:, j * cols:(j + 1) * cols], vj, preferred_element_type=F32)
        pv = part if pv is None else pv + part
    acc_ref[...] = alpha * acc_ref[...] + pv
    m_ref[...] = m_new

    @pl.when(step == pl.num_programs(1) - 1)
    def _():
        s_new = jnp.sum(q * knew_ref[0], axis=-1, keepdims=True)
        m_old2 = m_ref[...]
        m_fin = jnp.maximum(m_old2, s_new)
        alpha2 = jnp.exp2(m_old2 - m_fin)
        p_new = jnp.exp2(s_new - m_fin)
        l_fin = alpha2 * l_ref[...] + p_new
        v_new = vnew_ref[0]
        acc = (alpha2 * acc_ref[...] + p_new * jnp.concatenate([v_new, v_new], axis=0)) / l_fin
        lam = _lambda(lam_ref, lam_init)
        o = acc[0:N_HEADS, :] - lam * acc[N_HEADS:n_maps, :]
        o_ref[0] = _subln(o, g_ref[...], lam_init)


def _attn_decode(q, k_new, v_new, cache_k, cache_v, layer, page_table, lam4, g, *, pages_per_step, lam_init):
    rows = q.shape[0]
    n_pages = page_table.shape[1]
    page = cache_k.shape[2]
    npg = pages_per_step
    hw = 2 * HEAD_DIM
    n_maps = 2 * N_HEADS

    def page_spec(j, heads, width):
        return pl.BlockSpec((None, None, page, heads, width),
                            lambda b, s, pt: (layer, pt[b, s * npg + j], 0, 0, 0))

    qk_spec = pl.BlockSpec((1, n_maps, HEAD_DIM), lambda b, s, pt: (b, 0, 0))
    v_spec = pl.BlockSpec((1, N_HEADS, hw), lambda b, s, pt: (b, 0, 0))
    grid_spec = pltpu.PrefetchScalarGridSpec(
        num_scalar_prefetch=1,
        grid=(rows, n_pages // npg),
        in_specs=[
            pl.BlockSpec((4, HEAD_DIM), lambda b, s, pt: (0, 0)),
            pl.BlockSpec((1, hw), lambda b, s, pt: (0, 0)),
            qk_spec, qk_spec, v_spec,
        ] + [page_spec(j, n_maps, HEAD_DIM) for j in range(npg)] + [page_spec(j, N_HEADS, hw) for j in range(npg)],
        out_specs=v_spec,
        scratch_shapes=[
            pltpu.VMEM((n_maps, 1), F32),
            pltpu.VMEM((n_maps, 1), F32),
            pltpu.VMEM((n_maps, hw), F32),
        ],
    )
    return pl.pallas_call(
        functools.partial(_attn_decode_kernel, pages_per_step=npg, page=page, lam_init=lam_init),
        grid_spec=grid_spec,
        out_shape=jax.ShapeDtypeStruct((rows, N_HEADS, hw), F32),
        compiler_params=_params(2),
        name="attn_decode",
    )(page_table, lam4, g, q, k_new, v_new, *([cache_k] * npg), *([cache_v] * npg))


def _rope_tables(pos):
    half = HEAD_DIM // 2
    inv_freq = jnp.power(ROPE_THETA, -jnp.arange(half, dtype=F32) * (2.0 / HEAD_DIM))
    ang = pos.astype(F32)[:, None] * inv_freq[None, :]
    cos = jnp.cos(ang)
    sin = jnp.sin(ang)
    return jnp.concatenate([cos, cos], axis=-1), jnp.concatenate([-sin, sin], axis=-1)


def kernel(x_prompt, x_sample, state_conv, state_h, cache_k, cache_v, page_table, norm_mix_pre, norm_mix_post, norm_ffn_pre, norm_ffn_post, lru_w_in, lru_b_in, lru_conv_w, lru_conv_b, lru_w_gate_a, lru_b_gate_a, lru_w_gate_x, lru_b_gate_x, lru_a_param, lru_w_out, attn_w_qkv, attn_lambda_q1, attn_lambda_k1, attn_lambda_q2, attn_lambda_k2, attn_subln, attn_w_o, ffn_w_gate, ffn_w_up, ffn_w_down):
    bp, seq, d = x_prompt.shape
    bd, t_dec, _ = x_sample.shape
    assert d == D_MODEL and t_dec == 1
    depth = norm_mix_pre.shape[0]
    n_pages = page_table.shape[1]
    page = cache_k.shape[2]
    past = n_pages * page
    mp = bp * seq

    xp = x_prompt.reshape(mp, d)
    xs = x_sample.reshape(bd, d)
    row = lambda v: v.reshape(1, -1)

    conv_p, h_p, k_p, v_p = [], [], [], []
    conv_s, h_s, k_s, v_s = [], [], [], []
    for i in range(depth):
        j = i // N_MIXERS
        g_pre = row(norm_mix_pre[i])
        g_post = row(norm_mix_post[i])
        if i % N_MIXERS == 0:
            w_in = lru_w_in[j].astype(BF16)
            b_in = row(lru_b_in[j])
            w_out = lru_w_out[j].astype(BF16)
            wa = lru_w_gate_a[j].astype(BF16)
            wx = lru_w_gate_x[j].astype(BF16)
            ba = lru_b_gate_a[j].reshape(LRU_BLOCKS, 1, LRU_BLOCK)
            bx = lru_b_gate_x[j].reshape(LRU_BLOCKS, 1, LRU_BLOCK)
            gate_prm = (lru_conv_w[j], row(lru_conv_b[j]), wa, ba, wx, bx, row(lru_a_param[j]))
            proj_p = _lru_in_proj(xp, g_pre, w_in, b_in, tm=1024, tn=1024)
            hy_p, hl_p = _lru_seq(proj_p, jnp.zeros((bp, SUBLANES, LRU_WIDTH), F32), jnp.zeros((bp, 1, LRU_WIDTH), F32),
                                  *gate_prm, batch=bp, seq=seq, tt=256)
            conv_p.append(proj_p.reshape(bp, seq, 2 * LRU_WIDTH)[:, seq - (CONV_WIDTH - 1):, LRU_WIDTH:])
            h_p.append(hl_p.reshape(bp, LRU_WIDTH))
            xp = _out_proj_residual(hy_p, w_out, g_post, xp, tm=512)
            proj_s = _lru_in_proj(xs, g_pre, w_in, b_in, tm=bd, tn=1024)
            hy_s, hl_s = _lru_step(proj_s, state_conv[j].transpose(1, 0, 2), state_h[j], *gate_prm)
            conv_s.append(jnp.concatenate([state_conv[j][:, 1:], proj_s[:, None, LRU_WIDTH:]], axis=1))
            h_s.append(hl_s)
            xs = _out_proj_residual(hy_s, w_out, g_post, xs, tm=bd)
        else:
            lam_init = 0.8 - 0.6 * math.exp(-0.3 * i)
            w_qkv = attn_w_qkv[j].astype(BF16)
            w_o = attn_w_o[j].astype(BF16)
            lam4 = jnp.stack([attn_lambda_q1[j], attn_lambda_k1[j], attn_lambda_q2[j], attn_lambda_k2[j]])
            g_sub = row(attn_subln[j])
            cos_p, sin_p = _rope_tables(jnp.arange(seq))
            qkv_p, k4_p, v4_p = _qkv_proj(xp, g_pre, w_qkv, cos_p, sin_p, tm=1024, tn=1024)
            o = _attn_prompt(qkv_p, lam4, g_sub, batch=bp, seq=seq, tile=1024, lam_init=lam_init)
            k_p.append(k4_p.reshape(bp, seq, 2 * N_HEADS, HEAD_DIM))
            v_p.append(v4_p.reshape(bp, seq, N_HEADS, 2 * HEAD_DIM))
            xp = _out_proj_residual(o, w_o, g_post, xp, tm=512)
            cos_s, sin_s = _rope_tables(jnp.full((bd,), past))
            qkv_s, ks_new, vs_new = _qkv_proj(xs, g_pre, w_qkv, cos_s, sin_s, tm=bd, tn=1024)
            map_major = lambda a: a.reshape(bd, N_HEADS, 2, HEAD_DIM).transpose(0, 2, 1, 3).reshape(bd, 2 * N_HEADS, HEAD_DIM)
            os_ = _attn_decode(map_major(qkv_s[:, :d]), map_major(ks_new), vs_new,
                               cache_k, cache_v, j, page_table, lam4, g_sub, pages_per_step=8, lam_init=lam_init)
            k_s.append(ks_new.reshape(bd, 1, 2 * N_HEADS, HEAD_DIM))
            v_s.append(vs_new.reshape(bd, 1, N_HEADS, 2 * HEAD_DIM))
            xs = _out_proj_residual(os_.reshape(bd, d), w_o, g_post, xs, tm=bd)
        gf_pre = row(norm_ffn_pre[i])
        gf_post = row(norm_ffn_post[i])
        xp = _ffn(xp, gf_pre, ffn_w_gate, ffn_w_up, ffn_w_down, gf_post, i, tm=1024, tf=512)
        xs = _ffn(xs, gf_pre, ffn_w_gate, ffn_w_up, ffn_w_down, gf_post, i, tm=bd, tf=512)
    return (xp.reshape(bp, seq, d), xs.reshape(bd, 1, d), jnp.stack(conv_p), jnp.stack(h_p), jnp.stack(k_p), jnp.stack(v_p),
            jnp.stack(conv_s), jnp.stack(h_s), jnp.stack(k_s), jnp.stack(v_s))
```

```python
import functools
import math

import jax
import jax.numpy as jnp
from jax import lax
from jax.experimental import pallas as pl
from jax.experimental.pallas import tpu as pltpu

D_MODEL = 2048
LRU_WIDTH = D_MODEL
LRU_BLOCK = 256
LRU_BLOCKS = LRU_WIDTH // LRU_BLOCK
CONV_WIDTH = 4
LRU_C = 8.0
HEAD_DIM = 128
N_HEADS = D_MODEL // (2 * HEAD_DIM)
ROPE_THETA = 10000.0
NORM_EPS = 1e-6
SUBLN_EPS = 1e-5
N_MIXERS = 2
Q_SCALE = HEAD_DIM ** -0.5 * math.log2(math.e)

SUBLANES = 8
LANES = 128
VMEM_CAPACITY_BYTES = 64 * 1024 * 1024
VMEM_LIMIT_BYTES = VMEM_CAPACITY_BYTES - 4 * 1024 * 1024

BF16 = jnp.bfloat16
F32 = jnp.float32


def _params(n_grid_dims):
    return pltpu.CompilerParams(
        dimension_semantics=("arbitrary",) * n_grid_dims,
        vmem_limit_bytes=VMEM_LIMIT_BYTES,
    )


def _rms_scale(x, eps):
    return lax.rsqrt(jnp.mean(x * x, axis=-1, keepdims=True) + eps)


def _sigmoid(x):
    return 1.0 / (1.0 + jnp.exp(-x))


def _gelu_tanh(x):
    c = math.sqrt(2.0 / math.pi)
    return 0.5 * x * (1.0 + jnp.tanh(c * (x + 0.044715 * (x * x * x))))


def _softplus(z):
    return jnp.maximum(z, 0.0) + jnp.log1p(jnp.exp(-jnp.abs(z)))


def _norm_once(x_ref, g_ref, h_ref):
    @pl.when(pl.program_id(1) == 0)
    def _():
        x = x_ref[...]
        h_ref[...] = (x * _rms_scale(x, NORM_EPS) * g_ref[...]).astype(BF16)


def _lru_in_kernel(x_ref, g_ref, w_ref, b_ref, o_ref, h_ref, *, n_gate_tiles):
    _norm_once(x_ref, g_ref, h_ref)
    j = pl.program_id(1)
    proj = lambda: jnp.dot(h_ref[...], w_ref[...], preferred_element_type=F32) + b_ref[...]

    @pl.when(j < n_gate_tiles)
    def _():
        o_ref[...] = _gelu_tanh(proj())

    @pl.when(j >= n_gate_tiles)
    def _():
        o_ref[...] = proj()


def _lru_in_proj(x, g, w, b, *, tm, tn):
    m, d = x.shape
    n = w.shape[1]
    return pl.pallas_call(
        functools.partial(_lru_in_kernel, n_gate_tiles=LRU_WIDTH // tn),
        grid=(m // tm, n // tn),
        in_specs=[
            pl.BlockSpec((tm, d), lambda i, j: (i, 0)),
            pl.BlockSpec((1, d), lambda i, j: (0, 0)),
            pl.BlockSpec((d, tn), lambda i, j: (0, j)),
            pl.BlockSpec((1, tn), lambda i, j: (0, j)),
        ],
        out_specs=pl.BlockSpec((tm, tn), lambda i, j: (i, j)),
        out_shape=jax.ShapeDtypeStruct((m, n), F32),
        scratch_shapes=[pltpu.VMEM((tm, d), BF16)],
        compiler_params=_params(2),
        name="lru_in_proj",
    )(x, g, w, b)


def _rope(x, cos2, sin2):
    return x * cos2 + pltpu.roll(x, HEAD_DIM // 2, axis=1) * sin2


def _qkv_kernel(x_ref, g_ref, w_ref, cos_ref, sin_ref, o_ref, k4_ref, v4_ref, h_ref, *, tiles_per_part, heads_per_tile):
    _norm_once(x_ref, g_ref, h_ref)
    j = pl.program_id(1)
    tm = o_ref.shape[0]
    proj = lambda: jnp.dot(h_ref[...], w_ref[...], preferred_element_type=F32)

    def roped(acc, scale):
        cos2 = cos_ref[...]
        sin2 = sin_ref[...]
        parts = []
        for hd in range(heads_per_tile):
            r = _rope(acc[:, hd * HEAD_DIM:(hd + 1) * HEAD_DIM], cos2, sin2)
            parts.append(r if scale is None else r * scale)
        return jnp.concatenate(parts, axis=1)

    @pl.when(j < tiles_per_part)
    def _():
        o_ref[...] = roped(proj(), Q_SCALE)

    @pl.when((j >= tiles_per_part) & (j < 2 * tiles_per_part))
    def _():
        k = roped(proj(), None)
        o_ref[...] = k
        k4_ref[...] = k.reshape(tm, heads_per_tile, HEAD_DIM)

    v_heads = heads_per_tile // 2
    for t in range(tiles_per_part):
        @pl.when(j == 2 * tiles_per_part + t)
        def _(t=t):
            v = proj()
            o_ref[...] = v
            v4_ref[:, t * v_heads:(t + 1) * v_heads, :] = v.reshape(tm, v_heads, 2 * HEAD_DIM)


def _qkv_proj(x, g, w, cos2, sin2, *, tm, tn):
    m, d = x.shape
    n = w.shape[1]
    tpp = d // tn
    hpt = tn // HEAD_DIM
    n_pos_tiles = cos2.shape[0] // tm
    return pl.pallas_call(
        functools.partial(_qkv_kernel, tiles_per_part=tpp, heads_per_tile=hpt),
        grid=(m // tm, n // tn),
        in_specs=[
            pl.BlockSpec((tm, d), lambda i, j: (i, 0), pipeline_mode=pl.Buffered(1)),
            pl.BlockSpec((1, d), lambda i, j: (0, 0)),
            pl.BlockSpec((d, tn), lambda i, j: (0, j)),
            pl.BlockSpec((tm, HEAD_DIM), lambda i, j: (i % n_pos_tiles, 0)),
            pl.BlockSpec((tm, HEAD_DIM), lambda i, j: (i % n_pos_tiles, 0)),
        ],
        out_specs=[
            pl.BlockSpec((tm, tn), lambda i, j: (i, j)),
            pl.BlockSpec((tm, hpt, HEAD_DIM), lambda i, j: (i, jnp.clip(j - tpp, 0, tpp - 1), 0)),
            pl.BlockSpec((tm, N_HEADS, 2 * HEAD_DIM), lambda i, j: (i, 0, 0)),
        ],
        out_shape=[
            jax.ShapeDtypeStruct((m, n), F32),
            jax.ShapeDtypeStruct((m, 2 * N_HEADS, HEAD_DIM), F32),
            jax.ShapeDtypeStruct((m, N_HEADS, 2 * HEAD_DIM), F32),
        ],
        scratch_shapes=[pltpu.VMEM((tm, d), BF16)],
        compiler_params=_params(2),
        name="qkv_proj",
    )(x, g, w, cos2, sin2)


def _out_proj_kernel(a_ref, w_ref, g_ref, x_ref, o_ref):
    y = jnp.dot(a_ref[...].astype(BF16), w_ref[...], preferred_element_type=F32)
    o_ref[...] = x_ref[...] + y * _rms_scale(y, NORM_EPS) * g_ref[...]


def _out_proj_residual(a, w, g, x, *, tm):
    m, k = a.shape
    d = w.shape[1]
    return pl.pallas_call(
        _out_proj_kernel,
        grid=(m // tm,),
        in_specs=[
            pl.BlockSpec((tm, k), lambda i: (i, 0)),
            pl.BlockSpec((k, d), lambda i: (0, 0), pipeline_mode=pl.Buffered(1)),
            pl.BlockSpec((1, d), lambda i: (0, 0)),
            pl.BlockSpec((tm, d), lambda i: (i, 0)),
        ],
        out_specs=pl.BlockSpec((tm, d), lambda i: (i, 0)),
        out_shape=jax.ShapeDtypeStruct((m, d), F32),
        compiler_params=_params(1),
        name="out_proj_residual",
    )(a, w, g, x)


def _ffn_kernel(x_ref, gpre_ref, wg_ref, wu_ref, wd_ref, gpost_ref, o_ref, h_ref):
    f = pl.program_id(1)

    @pl.when(f == 0)
    def _():
        x = x_ref[...]
        h_ref[...] = (x * _rms_scale(x, NORM_EPS) * gpre_ref[...]).astype(BF16)

    def partial_out():
        h = h_ref[...]
        gate = jnp.dot(h, wg_ref[...].astype(BF16), preferred_element_type=F32)
        up = jnp.dot(h, wu_ref[...].astype(BF16), preferred_element_type=F32)
        act = (gate * _sigmoid(gate) * up).astype(BF16)
        return jnp.dot(act, wd_ref[...].astype(BF16), preferred_element_type=F32)

    @pl.when(f == 0)
    def _():
        o_ref[...] = partial_out()

    @pl.when(f > 0)
    def _():
        o_ref[...] += partial_out()

    @pl.when(f == pl.num_programs(1) - 1)
    def _():
        y = o_ref[...]
        o_ref[...] = x_ref[...] + y * _rms_scale(y, NORM_EPS) * gpost_ref[...]


def _ffn(x, gpre, wg, wu, wd, gpost, layer, *, tm, tf):
    m, d = x.shape
    dff = wg.shape[2]
    rows_once = dict(pipeline_mode=pl.Buffered(1))
    return pl.pallas_call(
        _ffn_kernel,
        grid=(m // tm, dff // tf),
        in_specs=[
            pl.BlockSpec((tm, d), lambda i, f: (i, 0), **rows_once),
            pl.BlockSpec((1, d), lambda i, f: (0, 0)),
            pl.BlockSpec((None, d, tf), lambda i, f: (layer, 0, f)),
            pl.BlockSpec((None, d, tf), lambda i, f: (layer, 0, f)),
            pl.BlockSpec((None, tf, d), lambda i, f: (layer, f, 0)),
            pl.BlockSpec((1, d), lambda i, f: (0, 0)),
        ],
        out_specs=pl.BlockSpec((tm, d), lambda i, f: (i, 0), **rows_once),
        out_shape=jax.ShapeDtypeStruct((m, d), F32),
        scratch_shapes=[pltpu.VMEM((tm, d), BF16)],
        compiler_params=_params(2),
        name="ffn",
    )(x, gpre, wg, wu, wd, gpost)


def _lru_gates(xc, wa, ba, wx, bx, sp):
    xcb = xc.astype(BF16)
    gate_a = _sigmoid(jnp.dot(xcb, wa, preferred_element_type=F32) + ba)
    gate_x = _sigmoid(jnp.dot(xcb, wx, preferred_element_type=F32) + bx)
    log_a = (-LRU_C * gate_a) * sp
    a = jnp.exp(log_a)
    y = -jnp.tanh(log_a) * (a * a + 1.0)
    b = jnp.where(y > 0.0, y * lax.rsqrt(y), 0.0) * (gate_x * xc)
    return a, b


def _lru_seq_kernel(xr_ref, yg_ref, conv0_ref, h0_ref, cw_ref, cb_ref, wa_ref, ba_ref, wx_ref, bx_ref, ap_ref,
                    o_ref, hlast_ref, xe_ref, h_ref, a_ref, b_ref, *, tt):
    ti = pl.program_id(1)
    pad = SUBLANES

    @pl.when(ti == 0)
    def _():
        xe_ref[0:pad, :] = conv0_ref[0]
        h_ref[...] = h0_ref[0]

    xe_ref[pad:pad + tt, :] = xr_ref[...]
    sp = _softplus(-ap_ref[...])
    groups = tt // SUBLANES
    rowmod = lax.broadcasted_iota(jnp.int32, (groups, SUBLANES, LRU_BLOCK), 1)

    for n in range(LRU_BLOCKS):
        cs = slice(n * LRU_BLOCK, (n + 1) * LRU_BLOCK)
        xg = xe_ref[:, cs].reshape(groups + 1, SUBLANES, LRU_BLOCK)
        xc = cb_ref[:, cs] + xg[1:].reshape(tt, LRU_BLOCK) * cw_ref[CONV_WIDTH - 1:CONV_WIDTH, cs]
        for k in range(1, CONV_WIDTH):
            rot = pltpu.roll(xg, k, axis=1)
            shifted = jnp.where(rowmod >= k, rot[1:], rot[:-1]).reshape(tt, LRU_BLOCK)
            xc = xc + shifted * cw_ref[CONV_WIDTH - 1 - k:CONV_WIDTH - k, cs]
        a, b = _lru_gates(xc, wa_ref[n], ba_ref[n], wx_ref[n], bx_ref[n], sp[:, cs])
        a = a.reshape(tt // SUBLANES, SUBLANES, LRU_BLOCK)
        b = b.reshape(tt // SUBLANES, SUBLANES, LRU_BLOCK)
        d = 1
        while d < SUBLANES:
            valid = rowmod >= d
            a_sh = jnp.where(valid, pltpu.roll(a, d, axis=1), 1.0)
            b_sh = jnp.where(valid, pltpu.roll(b, d, axis=1), 0.0)
            b = a * b_sh + b
            a = a * a_sh
            d *= 2
        a_ref[:, cs] = a.reshape(tt, LRU_BLOCK)
        b_ref[:, cs] = b.reshape(tt, LRU_BLOCK)

    def group(gi, carry):
        r = pl.multiple_of(gi * SUBLANES, SUBLANES)
        h = a_ref[pl.ds(r, SUBLANES), :] * carry + b_ref[pl.ds(r, SUBLANES), :]
        b_ref[pl.ds(r, SUBLANES), :] = h
        return h[SUBLANES - 1:SUBLANES, :]

    carry = lax.fori_loop(0, tt // SUBLANES, group, h_ref[...], unroll=4)
    h_ref[...] = carry
    hlast_ref[0] = carry
    o_ref[...] = (b_ref[...] * yg_ref[...]).astype(o_ref.dtype)
    xe_ref[0:pad, :] = xe_ref[tt:tt + pad, :]


def _lru_seq(proj, conv0, h0, cw, cb, wa, ba, wx, bx, ap, *, batch, seq, tt):
    w = LRU_WIDTH
    nt = seq // tt
    const2 = lambda b, t: (0, 0)
    const3 = lambda b, t: (0, 0, 0)
    return pl.pallas_call(
        functools.partial(_lru_seq_kernel, tt=tt),
        grid=(batch, nt),
        in_specs=[
            pl.BlockSpec((tt, w), lambda b, t: (b * nt + t, 1)),
            pl.BlockSpec((tt, w), lambda b, t: (b * nt + t, 0)),
            pl.BlockSpec((1, SUBLANES, w), lambda b, t: (b, 0, 0)),
            pl.BlockSpec((1, 1, w), lambda b, t: (b, 0, 0)),
            pl.BlockSpec((CONV_WIDTH, w), const2),
            pl.BlockSpec((1, w), const2),
            pl.BlockSpec((LRU_BLOCKS, LRU_BLOCK, LRU_BLOCK), const3),
            pl.BlockSpec((LRU_BLOCKS, 1, LRU_BLOCK), const3),
            pl.BlockSpec((LRU_BLOCKS, LRU_BLOCK, LRU_BLOCK), const3),
            pl.BlockSpec((LRU_BLOCKS, 1, LRU_BLOCK), const3),
            pl.BlockSpec((1, w), const2),
        ],
        out_specs=[
            pl.BlockSpec((tt, w), lambda b, t: (b * nt + t, 0)),
            pl.BlockSpec((1, 1, w), lambda b, t: (b, 0, 0)),
        ],
        out_shape=[
            jax.ShapeDtypeStruct((batch * seq, w), BF16),
            jax.ShapeDtypeStruct((batch, 1, w), F32),
        ],
        scratch_shapes=[
            pltpu.VMEM((tt + SUBLANES, w), F32),
            pltpu.VMEM((1, w), F32),
            pltpu.VMEM((tt, w), F32),
            pltpu.VMEM((tt, w), F32),
        ],
        compiler_params=_params(2),
        name="lru_seq",
    )(proj, proj, conv0, h0, cw, cb, wa, ba, wx, bx, ap)


def _lru_step_kernel(xr_ref, yg_ref, conv_ref, h0_ref, cw_ref, cb_ref, wa_ref, ba_ref, wx_ref, bx_ref, ap_ref,
                     o_ref, hnew_ref):
    sp = _softplus(-ap_ref[...])
    for n in range(LRU_BLOCKS):
        cs = slice(n * LRU_BLOCK, (n + 1) * LRU_BLOCK)
        xc = cb_ref[:, cs] + conv_ref[0, :, cs] * cw_ref[0:1, cs]
        for tap in range(1, CONV_WIDTH - 1):
            xc = xc + conv_ref[tap, :, cs] * cw_ref[tap:tap + 1, cs]
        xc = xc + xr_ref[:, cs] * cw_ref[CONV_WIDTH - 1:CONV_WIDTH, cs]
        a, b = _lru_gates(xc, wa_ref[n], ba_ref[n], wx_ref[n], bx_ref[n], sp[:, cs])
        h = a * h0_ref[:, cs] + b
        hnew_ref[:, cs] = h
        o_ref[:, cs] = h * yg_ref[:, cs]


def _lru_step(proj, conv_t, h0, cw, cb, wa, ba, wx, bx, ap):
    m = proj.shape[0]
    w = LRU_WIDTH
    c2 = lambda i: (0, 0)
    c3 = lambda i: (0, 0, 0)
    return pl.pallas_call(
        _lru_step_kernel,
        grid=(1,),
        in_specs=[
            pl.BlockSpec((m, w), lambda i: (0, 1)),
            pl.BlockSpec((m, w), lambda i: (0, 0)),
            pl.BlockSpec((CONV_WIDTH - 1, m, w), c3),
            pl.BlockSpec((m, w), c2),
            pl.BlockSpec((CONV_WIDTH, w), c2),
            pl.BlockSpec((1, w), c2),
            pl.BlockSpec((LRU_BLOCKS, LRU_BLOCK, LRU_BLOCK), c3),
            pl.BlockSpec((LRU_BLOCKS, 1, LRU_BLOCK), c3),
            pl.BlockSpec((LRU_BLOCKS, LRU_BLOCK, LRU_BLOCK), c3),
            pl.BlockSpec((LRU_BLOCKS, 1, LRU_BLOCK), c3),
            pl.BlockSpec((1, w), c2),
        ],
        out_specs=[pl.BlockSpec((m, w), c2), pl.BlockSpec((m, w), c2)],
        out_shape=[jax.ShapeDtypeStruct((m, w), F32), jax.ShapeDtypeStruct((m, w), F32)],
        compiler_params=_params(1),
        name="lru_step",
    )(proj, proj, conv_t, h0, cw, cb, wa, ba, wx, bx, ap)


def _lambda(lam_ref, lam_init):
    s1 = jnp.sum(lam_ref[0:1, :] * lam_ref[1:2, :], axis=-1, keepdims=True)
    s2 = jnp.sum(lam_ref[2:3, :] * lam_ref[3:4, :], axis=-1, keepdims=True)
    return jnp.exp(s1) - jnp.exp(s2) + lam_init


def _subln(o, g, lam_init):
    return o * _rms_scale(o, SUBLN_EPS) * g * (1.0 - lam_init)


def _nt_dot(a, b):
    return lax.dot_general(a, b, (((1,), (1,)), ((), ())), preferred_element_type=F32)


def _attn_prompt_kernel(lam_ref, g_ref, q_ref, k_ref, v_ref, o_ref, qb_ref, kb_ref, vb_ref, m_ref, l_ref, acc_ref,
                        *, tile, lam_init):
    maps = range(2)
    hs = [slice(s * HEAD_DIM, (s + 1) * HEAD_DIM) for s in maps]
    seq = q_ref.shape[0]
    half = tile // 2
    qb_ref[...] = q_ref[...].astype(BF16)
    kb_ref[...] = k_ref[...].astype(BF16)
    vb_ref[...] = v_ref[...].astype(BF16)
    lam = _lambda(lam_ref, lam_init)

    def lane_groups(x):
        return [x[:, j * LANES:(j + 1) * LANES] for j in range(x.shape[1] // LANES)]

    def scores(q0, rows, k0, klen, diagonal):
        kc = kb_ref[pl.ds(k0, klen), :]
        sc = [_nt_dot(qb_ref[q0:q0 + rows, hs[s]], kc[:, hs[s]]) for s in maps]
        if diagonal:
            keep = (lax.broadcasted_iota(jnp.int32, (rows, klen), 1)
                    <= lax.broadcasted_iota(jnp.int32, (rows, klen), 0))
            sc = [jnp.where(keep, x, -jnp.inf) for x in sc]
        return sc

    for qi in range(seq // tile):
        base = qi * tile

        def max_block(q0, rows, k0, klen, diagonal, base=base):
            sc = scores(q0, rows, k0, klen, diagonal)
            r = slice(q0 - base, q0 - base + rows)
            m_old = [m_ref[s, r] for s in maps]
            for s in maps:
                m_ref[s, r] = functools.reduce(jnp.maximum, lane_groups(sc[s]), m_old[s])

        def sum_block(q0, rows, k0, klen, diagonal, base=base):
            sc = scores(q0, rows, k0, klen, diagonal)
            vc = vb_ref[pl.ds(k0, klen), :]
            r = slice(q0 - base, q0 - base + rows)
            m_row = [m_ref[s, r] for s in maps]
            l_old = [l_ref[s, r] for s in maps]
            acc_old = [acc_ref[s, r] for s in maps]
            p = [[jnp.exp2(x - m_row[s]) for x in lane_groups(sc[s])] for s in maps]
            for s in maps:
                l_ref[s, r] = functools.reduce(jnp.add, p[s], l_old[s])
                pb = jnp.concatenate(p[s], axis=1).astype(BF16)
                acc_ref[s, r] = acc_old[s] + jnp.dot(pb, vc, preferred_element_type=F32)

        def sweep(block, qi=qi, base=base):
            def left(c, carry):
                block(base, tile, pl.multiple_of(c * tile, tile), tile, False)
                return carry
            if qi:
                lax.fori_loop(0, qi, left, 0)
            block(base, half, base, half, True)
            block(base + half, half, base, half, False)
            block(base + half, half, base + half, half, True)

        m_ref[...] = jnp.full(m_ref.shape, -jnp.inf, F32)
        sweep(max_block)
        for s in maps:
            m_ref[s] = jnp.broadcast_to(jnp.max(m_ref[s], axis=-1, keepdims=True), (tile, LANES))
        l_ref[...] = jnp.zeros(l_ref.shape, F32)
        acc_ref[...] = jnp.zeros(acc_ref.shape, F32)
        sweep(sum_block)
        inv = [1.0 / jnp.sum(l_ref[s], axis=-1, keepdims=True) for s in maps]
        o = acc_ref[0] * inv[0] - lam * (acc_ref[1] * inv[1])
        o_ref[base:base + tile, :] = _subln(o, g_ref[...], lam_init).astype(o_ref.dtype)


def _attn_prompt(qkv, lam4, g, *, batch, seq, tile, lam_init):
    hw = 2 * HEAD_DIM
    part = D_MODEL // hw
    return pl.pallas_call(
        functools.partial(_attn_prompt_kernel, tile=tile, lam_init=lam_init),
        grid=(batch, N_HEADS),
        in_specs=[
            pl.BlockSpec((4, HEAD_DIM), lambda b, h: (0, 0)),
            pl.BlockSpec((1, hw), lambda b, h: (0, 0)),
            pl.BlockSpec((seq, hw), lambda b, h: (b, h)),
            pl.BlockSpec((seq, hw), lambda b, h: (b, part + h)),
            pl.BlockSpec((seq, hw), lambda b, h: (b, 2 * part + h)),
        ],
        out_specs=pl.BlockSpec((seq, hw), lambda b, h: (b, h)),
        out_shape=jax.ShapeDtypeStruct((batch * seq, D_MODEL), BF16),
        scratch_shapes=[
            pltpu.VMEM((seq, hw), BF16),
            pltpu.VMEM((seq, hw), BF16),
            pltpu.VMEM((seq, hw), BF16),
            pltpu.VMEM((2, tile, LANES), F32),
            pltpu.VMEM((2, tile, LANES), F32),
            pltpu.VMEM((2, tile, hw), F32),
        ],
        compiler_params=_params(2),
        name="attn_prompt",
    )(lam4, g, qkv, qkv, qkv)


def _attn_decode_kernel(pt_ref, lam_ref, g_ref, q_ref, knew_ref, vnew_ref, *rest, pages_per_step, page, lam_init):
    del pt_ref
    npg = pages_per_step
    k_refs = rest[:npg]
    v_refs = rest[npg:2 * npg]
    o_ref, m_ref, l_ref, acc_ref = rest[2 * npg:]
    step = pl.program_id(1)
    n_maps = 2 * N_HEADS
    cols = page * N_HEADS

    @pl.when(step == 0)
    def _():
        m_ref[...] = jnp.full(m_ref.shape, -jnp.inf, F32)
        l_ref[...] = jnp.zeros(l_ref.shape, F32)
        acc_ref[...] = jnp.zeros(acc_ref.shape, F32)

    q = q_ref[0]
    qb = q.astype(BF16)
    row = lax.broadcasted_iota(jnp.int32, (n_maps, cols), 0)
    col = lax.broadcasted_iota(jnp.int32, (n_maps, cols), 1)
    first_map = row < N_HEADS
    own = (col & (N_HEADS - 1)) == (row & (N_HEADS - 1))

    scores = []
    for j in range(npg):
        k_even = k_refs[j][:, pl.ds(0, N_HEADS, stride=2), :].reshape(cols, HEAD_DIM).astype(BF16)
        k_odd = k_refs[j][:, pl.ds(1, N_HEADS, stride=2), :].reshape(cols, HEAD_DIM).astype(BF16)
        sc = jnp.where(first_map, _nt_dot(qb, k_even), _nt_dot(qb, k_odd))
        scores.append(jnp.where(own, sc, -jnp.inf))
    sc = jnp.concatenate(scores, axis=1) if npg > 1 else scores[0]
    m_old = m_ref[...]
    m_new = jnp.maximum(m_old, jnp.max(sc, axis=-1, keepdims=True))
    alpha = jnp.exp2(m_old - m_new)
    p = jnp.exp2(sc - m_new)
    l_ref[...] = alpha * l_ref[...] + jnp.sum(p, axis=-1, keepdims=True)
    pb = p.astype(BF16)
    pv = None
    for j in range(npg):
        vj = v_refs[j][...].reshape(cols, 2 * HEAD_DIM).astype(BF16)
        part = jnp.dot(pb[:, j * cols:(j + 1) * cols], vj, preferred_element_type=F32)
        pv = part if pv is None else pv + part
    acc_ref[...] = alpha * acc_ref[...] + pv
    m_ref[...] = m_new

    @pl.when(step == pl.num_programs(1) - 1)
    def _():
        s_new = jnp.sum(q * knew_ref[0], axis=-1, keepdims=True)
        m_old2 = m_ref[...]
        m_fin = jnp.maximum(m_old2, s_new)
        alpha2 = jnp.exp2(m_old2 - m_fin)
        p_new = jnp.exp2(s_new - m_fin)
        l_fin = alpha2 * l_ref[...] + p_new
        v_new = vnew_ref[0]
        acc = (alpha2 * acc_ref[...] + p_new * jnp.concatenate([v_new, v_new], axis=0)) / l_fin
        lam = _lambda(lam_ref, lam_init)
        o = acc[0:N_HEADS, :] - lam * acc[N_HEADS:n_maps, :]
        o_ref[0] = _subln(o, g_ref[...], lam_init)


def _attn_decode(q, k_new, v_new, cache_k, cache_v, layer, page_table, lam4, g, *, pages_per_step, lam_init):
    rows = q.shape[0]
    n_pages = page_table.shape[1]
    page = cache_k.shape[2]
    npg = pages_per_step
    hw = 2 * HEAD_DIM
    n_maps = 2 * N_HEADS

    def page_spec(j, heads, width):
        return pl.BlockSpec((None, None, page, heads, width),
                            lambda b, s, pt: (layer, pt[b, s * npg + j], 0, 0, 0))

    qk_spec = pl.BlockSpec((1, n_maps, HEAD_DIM), lambda b, s, pt: (b, 0, 0))
    v_spec = pl.BlockSpec((1, N_HEADS, hw), lambda b, s, pt: (b, 0, 0))
    grid_spec = pltpu.PrefetchScalarGridSpec(
        num_scalar_prefetch=1,
        grid=(rows, n_pages // npg),
        in_specs=[
            pl.BlockSpec((4, HEAD_DIM), lambda b, s, pt: (0, 0)),
            pl.BlockSpec((1, hw), lambda b, s, pt: (0, 0)),
            qk_spec, qk_spec, v_spec,
        ] + [page_spec(j, n_maps, HEAD_DIM) for j in range(npg)] + [page_spec(j, N_HEADS, hw) for j in range(npg)],
        out_specs=v_spec,
        scratch_shapes=[
            pltpu.VMEM((n_maps, 1), F32),
            pltpu.VMEM((n_maps, 1), F32),
            pltpu.VMEM((n_maps, hw), F32),
        ],
    )
    return pl.pallas_call(
        functools.partial(_attn_decode_kernel, pages_per_step=npg, page=page, lam_init=lam_init),
        grid_spec=grid_spec,
        out_shape=jax.ShapeDtypeStruct((rows, N_HEADS, hw), F32),
        compiler_params=_params(2),
        name="attn_decode",
    )(page_table, lam4, g, q, k_new, v_new, *([cache_k] * npg), *([cache_v] * npg))


def _rope_tables(pos):
    half = HEAD_DIM // 2
    inv_freq = jnp.power(ROPE_THETA, -jnp.arange(half, dtype=F32) * (2.0 / HEAD_DIM))
    ang = pos.astype(F32)[:, None] * inv_freq[None, :]
    cos = jnp.cos(ang)
    sin = jnp.sin(ang)
    return jnp.concatenate([cos, cos], axis=-1), jnp.concatenate([-sin, sin], axis=-1)


def kernel(x_prompt, x_sample, state_conv, state_h, cache_k, cache_v, page_table, norm_mix_pre, norm_mix_post, norm_ffn_pre, norm_ffn_post, lru_w_in, lru_b_in, lru_conv_w, lru_conv_b, lru_w_gate_a, lru_b_gate_a, lru_w_gate_x, lru_b_gate_x, lru_a_param, lru_w_out, attn_w_qkv, attn_lambda_q1, attn_lambda_k1, attn_lambda_q2, attn_lambda_k2, attn_subln, attn_w_o, ffn_w_gate, ffn_w_up, ffn_w_down):
    bp, seq, d = x_prompt.shape
    bd, t_dec, _ = x_sample.shape
    assert d == D_MODEL and t_dec == 1
    depth = norm_mix_pre.shape[0]
    n_pages = page_table.shape[1]
    page = cache_k.shape[2]
    past = n_pages * page
    mp = bp * seq

    xp = x_prompt.reshape(mp, d)
    xs = x_sample.reshape(bd, d)
    row = lambda v: v.reshape(1, -1)

    conv_p, h_p, k_p, v_p = [], [], [], []
    conv_s, h_s, k_s, v_s = [], [], [], []
    for i in range(depth):
        j = i // N_MIXERS
        g_pre = row(norm_mix_pre[i])
        g_post = row(norm_mix_post[i])
        if i % N_MIXERS == 0:
            w_in = lru_w_in[j].astype(BF16)
            b_in = row(lru_b_in[j])
            w_out = lru_w_out[j].astype(BF16)
            wa = lru_w_gate_a[j].astype(BF16)
            wx = lru_w_gate_x[j].astype(BF16)
            ba = lru_b_gate_a[j].reshape(LRU_BLOCKS, 1, LRU_BLOCK)
            bx = lru_b_gate_x[j].reshape(LRU_BLOCKS, 1, LRU_BLOCK)
            gate_prm = (lru_conv_w[j], row(lru_conv_b[j]), wa, ba, wx, bx, row(lru_a_param[j]))
            proj_p = _lru_in_proj(xp, g_pre, w_in, b_in, tm=1024, tn=2048)
            hy_p, hl_p = _lru_seq(proj_p, jnp.zeros((bp, SUBLANES, LRU_WIDTH), F32), jnp.zeros((bp, 1, LRU_WIDTH), F32),
                                  *gate_prm, batch=bp, seq=seq, tt=256)
            conv_p.append(proj_p.reshape(bp, seq, 2 * LRU_WIDTH)[:, seq - (CONV_WIDTH - 1):, LRU_WIDTH:])
            h_p.append(hl_p.reshape(bp, LRU_WIDTH))
            xp = _out_proj_residual(hy_p, w_out, g_post, xp, tm=1024)
            proj_s = _lru_in_proj(xs, g_pre, w_in, b_in, tm=bd, tn=1024)
            hy_s, hl_s = _lru_step(proj_s, state_conv[j].transpose(1, 0, 2), state_h[j], *gate_prm)
            conv_s.append(jnp.concatenate([state_conv[j][:, 1:], proj_s[:, None, LRU_WIDTH:]], axis=1))
            h_s.append(hl_s)
            xs = _out_proj_residual(hy_s, w_out, g_post, xs, tm=bd)
        else:
            lam_init = 0.8 - 0.6 * math.exp(-0.3 * i)
            w_qkv = attn_w_qkv[j].astype(BF16)
            w_o = attn_w_o[j].astype(BF16)
            lam4 = jnp.stack([attn_lambda_q1[j], attn_lambda_k1[j], attn_lambda_q2[j], attn_lambda_k2[j]])
            g_sub = row(attn_subln[j])
            cos_p, sin_p = _rope_tables(jnp.arange(seq))
            qkv_p, k4_p, v4_p = _qkv_proj(xp, g_pre, w_qkv, cos_p, sin_p, tm=1024, tn=1024)
            o = _attn_prompt(qkv_p, lam4, g_sub, batch=bp, seq=seq, tile=1024, lam_init=lam_init)
            k_p.append(k4_p.reshape(bp, seq, 2 * N_HEADS, HEAD_DIM))
            v_p.append(v4_p.reshape(bp, seq, N_HEADS, 2 * HEAD_DIM))
            xp = _out_proj_residual(o, w_o, g_post, xp, tm=1024)
            cos_s, sin_s = _rope_tables(jnp.full((bd,), past))
            qkv_s, ks_new, vs_new = _qkv_proj(xs, g_pre, w_qkv, cos_s, sin_s, tm=bd, tn=1024)
            map_major = lambda a: a.reshape(bd, N_HEADS, 2, HEAD_DIM).transpose(0, 2, 1, 3).reshape(bd, 2 * N_HEADS, HEAD_DIM)
            os_ = _attn_decode(map_major(qkv_s[:, :d]), map_major(ks_new), vs_new,
                               cache_k, cache_v, j, page_table, lam4, g_sub, pages_per_step=8, lam_init=lam_init)
            k_s.append(ks_new.reshape(bd, 1, 2 * N_HEADS, HEAD_DIM))
            v_s.append(vs_new.reshape(bd, 1, N_HEADS, 2 * HEAD_DIM))
            xs = _out_proj_residual(os_.reshape(bd, d), w_o, g_post, xs, tm=bd)
        gf_pre = row(norm_ffn_pre[i])
        gf_post = row(norm_ffn_post[i])
        xp = _ffn(xp, gf_pre, ffn_w_gate, ffn_w_up, ffn_w_down, gf_post, i, tm=1024, tf=512)
        xs = _ffn(xs, gf_pre, ffn_w_gate, ffn_w_up, ffn_w_down, gf_post, i, tm=bd, tf=512)
    return (xp.reshape(bp, seq, d), xs.reshape(bd, 1, d), jnp.stack(conv_p), jnp.stack(h_p), jnp.stack(k_p), jnp.stack(v_p),
            jnp.stack(conv_s), jnp.stack(h_s), jnp.stack(k_s), jnp.stack(v_s))
```
